```python
import jax, jax.numpy as jnp
from jax import lax
import numpy as np

D_MODEL = 4096
BATCH = 1
SEQ = 8192
DEPTH = 2

HEAD_DIM = 128
MIX_WIDTH = D_MODEL // 2
A_WIDTH = MIX_WIDTH
A_HEADS = A_WIDTH // HEAD_DIM
A_CONV_WIDTH = 4
A_C = 8.0
B_HEADS = MIX_WIDTH // HEAD_DIM
B_WIDTH = B_HEADS * HEAD_DIM
B_BLOCK = 128
C_HEADS = MIX_WIDTH // HEAD_DIM
C_DK = HEAD_DIM
C_DV = HEAD_DIM
C_WIDTH = C_HEADS * C_DK
C_LOW_RANK = 16
C_GATE_TEMP = 16.0
C_CHUNK = 64
N_BRANCH = 3
IN_SIZES = (A_WIDTH, A_WIDTH, B_WIDTH, B_WIDTH, B_WIDTH, C_WIDTH, C_WIDTH, C_WIDTH, C_WIDTH, C_LOW_RANK, N_BRANCH * D_MODEL)
IN_WIDTH = sum(IN_SIZES)
MEM_LEN = 256
X_HEADS = 4
X_HEAD_DIM = 256
X_WIDTH = X_HEADS * X_HEAD_DIM
PEER_HEADS = 8
PEER_QDIM = 256
PEER_NKEYS = 128
PEER_N = PEER_NKEYS * PEER_NKEYS
PEER_TOPK = 16
PEER_CHUNK = 128
EPS = 1e-6

kernel_name = 'hybrid_rglru_stickbreak_gla_peer_block'


def rms_norm(x, g):
    xf = x.astype(jnp.float32)
    y = xf * lax.rsqrt(jnp.mean(xf * xf, axis=-1, keepdims=True) + EPS)
    return (y * g.astype(jnp.float32)).astype(x.dtype)


def causal_depthwise_conv(x, w, b):
    y = lax.conv_general_dilated(x, w[:, None, :].astype(x.dtype), window_strides=(1,), padding=[(A_CONV_WIDTH - 1, 0)], dimension_numbers=('NWC', 'WIO', 'NWC'), feature_group_count=x.shape[-1])
    return y + b.astype(x.dtype)


def rg_lru(x, w_r, b_r, w_i, b_i, lam):
    B_, S_, _ = x.shape
    xh = x.reshape(B_, S_, A_HEADS, HEAD_DIM)
    r = jax.nn.sigmoid((jnp.einsum('bshi,hij->bshj', xh, w_r).reshape(B_, S_, A_WIDTH) + b_r).astype(jnp.float32))
    i = jax.nn.sigmoid((jnp.einsum('bshi,hij->bshj', xh, w_i).reshape(B_, S_, A_WIDTH) + b_i).astype(jnp.float32))
    log_a = -A_C * r * jax.nn.softplus(-lam.astype(jnp.float32))
    a = jnp.exp(log_a)
    u = jnp.sqrt(-jnp.expm1(2.0 * log_a)) * (i * x.astype(jnp.float32))

    def combine(left, right):
        a1, b1 = left
        a2, b2 = right
        return a1 * a2, a2 * b1 + b2

    _, h = lax.associative_scan(combine, (a, u), axis=1)
    return h.astype(x.dtype)


def stick_breaking_attention(q, k, v):
    B_, S_, H, dh = q.shape
    nb = S_ // B_BLOCK
    qb = q.reshape(B_, nb, B_BLOCK, H, dh).transpose(1, 0, 3, 2, 4)
    kt = k.transpose(0, 2, 1, 3)
    vt = v.transpose(0, 2, 1, 3)
    key_pos = jnp.arange(S_)
    scale = dh ** -0.5

    def block(args):
        q_blk, blk = args
        z = jnp.einsum('bhqd,bhkd->bhqk', q_blk, kt).astype(jnp.float32) * scale
        q_pos = blk * B_BLOCK + jnp.arange(B_BLOCK)
        mask = key_pos[None, :] < q_pos[:, None]
        log_keep = jnp.where(mask, jax.nn.log_sigmoid(-z), 0.0)
        after = lax.cumsum(log_keep, axis=3, reverse=True) - log_keep
        w = jnp.where(mask, jnp.exp(jax.nn.log_sigmoid(z) + after), 0.0)
        return jnp.einsum('bhqk,bhkd->bhqd', w.astype(vt.dtype), vt)

    o = lax.map(block, (qb, jnp.arange(nb)))
    return o.transpose(1, 0, 3, 2, 4).reshape(B_, S_, H * dh)


def gla_chunked(q, k, v, log_g):
    B_, S_, H, dk = q.shape
    n = S_ // C_CHUNK

    def chunks(t):
        return t.reshape(B_, n, C_CHUNK, H, t.shape[-1]).transpose(1, 0, 3, 2, 4).astype(jnp.float32)

    qc = chunks(q) * (dk ** -0.5)
    kc = chunks(k)
    vc = chunks(v)
    b = jnp.cumsum(chunks(log_g), axis=3)
    b_last = b[:, :, :, -1:, :]
    q_dec = qc * jnp.exp(b)
    k_in = kc * jnp.exp(-b)
    k_state = kc * jnp.exp(b_last - b)
    tril = jnp.tril(jnp.ones((C_CHUNK, C_CHUNK), dtype=bool))
    p = jnp.where(tril, jnp.einsum('nbhid,nbhjd->nbhij', q_dec, k_in), 0.0)
    o_intra = jnp.einsum('nbhij,nbhjd->nbhid', p, vc)

    def step(state, xs):
        qd, ks, vv, bl = xs
        o = jnp.einsum('bhcd,bhde->bhce', qd, state)
        state = jnp.exp(bl[:, :, 0, :])[..., None] * state + jnp.einsum('bhcd,bhce->bhde', ks, vv)
        return state, o

    state0 = jnp.zeros((B_, H, dk, v.shape[-1]), jnp.float32)
    _, o_inter = lax.scan(step, state0, (q_dec, k_state, vc, b_last))
    o = o_intra + o_inter
    return o.transpose(1, 0, 3, 2, 4).reshape(B_, S_, H, v.shape[-1])


def hybrid_mixer(h, w_in, conv_w, conv_b, lru_wr, lru_br, lru_wi, lru_bi, lru_lambda, gla_w_lr, gla_b_gate, gla_norm, w_pa, w_pb, w_pc, w_out):
    B_, S_, D = h.shape
    proj = jnp.einsum('bsd,de->bse', h, w_in)
    offsets = [int(o) for o in np.cumsum(IN_SIZES)[:-1]]
    a_x, a_gate, b_q, b_k, b_v, c_q, c_k, c_v, c_gate, c_lr, merge = jnp.split(proj, offsets, axis=-1)
    y_a = rg_lru(causal_depthwise_conv(a_x, conv_w, conv_b), lru_wr, lru_br, lru_wi, lru_bi, lru_lambda) * jax.nn.gelu(a_gate)
    hd = (B_, S_, B_HEADS, HEAD_DIM)
    y_b = stick_breaking_attention(b_q.reshape(hd), b_k.reshape(hd), b_v.reshape(hd)).astype(h.dtype)
    log_g = jax.nn.log_sigmoid((jnp.einsum('bsr,re->bse', c_lr, gla_w_lr) + gla_b_gate).astype(jnp.float32)) / C_GATE_TEMP
    ch = (B_, S_, C_HEADS, C_DK)
    o_c = gla_chunked(c_q.reshape(ch), c_k.reshape(ch), c_v.reshape(B_, S_, C_HEADS, C_DV), log_g.reshape(ch))
    y_c = rms_norm(o_c, gla_norm).reshape(B_, S_, C_WIDTH).astype(h.dtype) * jax.nn.silu(c_gate)
    g = jax.nn.sigmoid(merge.astype(jnp.float32)).reshape(B_, S_, N_BRANCH, D).astype(h.dtype)
    mixed = (g[:, :, 0] * jnp.einsum('bsw,wd->bsd', y_a, w_pa)
             + g[:, :, 1] * jnp.einsum('bsw,wd->bsd', y_b, w_pb)
             + g[:, :, 2] * jnp.einsum('bsw,wd->bsd', y_c, w_pc))
    return jnp.einsum('bsd,de->bse', mixed, w_out)


def memory_cross_attention(h, mem_n, wq, wk, wv, wo):
    B_, S_, _ = h.shape
    M = mem_n.shape[1]
    q = jnp.einsum('bsd,de->bse', h, wq).reshape(B_, S_, X_HEADS, X_HEAD_DIM)
    k = jnp.einsum('bmd,de->bme', mem_n, wk).reshape(B_, M, X_HEADS, X_HEAD_DIM)
    v = jnp.einsum('bmd,de->bme', mem_n, wv).reshape(B_, M, X_HEADS, X_HEAD_DIM)
    s = jnp.einsum('bshd,bmhd->bhsm', q, k).astype(jnp.float32) * (X_HEAD_DIM ** -0.5)
    p = jax.nn.softmax(s, axis=-1).astype(v.dtype)
    o = jnp.einsum('bhsm,bmhd->bshd', p, v).reshape(B_, S_, X_WIDTH)
    return jnp.einsum('bse,ed->bsd', o, wo)


def peer_ffn(h, w_query, sub_keys, expert_down, expert_up):
    B_, S_, D = h.shape
    T = B_ * S_
    ht = h.reshape(T, D)
    q = jnp.einsum('td,de->te', ht, w_query).reshape(T, PEER_HEADS, 2, PEER_QDIM // 2).astype(jnp.float32)
    s = jnp.einsum('thpd,hpkd->thpk', q, sub_keys.astype(jnp.float32))
    top_s, top_i = lax.top_k(s, PEER_TOPK)
    cand_s = (top_s[:, :, 0, :, None] + top_s[:, :, 1, None, :]).reshape(T, PEER_HEADS, PEER_TOPK * PEER_TOPK)
    cand_i = (top_i[:, :, 0, :, None] * PEER_NKEYS + top_i[:, :, 1, None, :]).reshape(T, PEER_HEADS, PEER_TOPK * PEER_TOPK)
    best_s, best_pos = lax.top_k(cand_s, PEER_TOPK)
    idx = jnp.take_along_axis(cand_i, best_pos, axis=-1)
    gate = jax.nn.softmax(best_s, axis=-1)
    nc = T // PEER_CHUNK

    def chunk(args):
        hc, ic, gc = args
        u = expert_down[ic]
        vv = expert_up[ic]
        act = jax.nn.gelu(jnp.einsum('td,thkd->thk', hc, u).astype(jnp.float32))
        return jnp.einsum('thk,thkd->td', (gc * act).astype(vv.dtype), vv)

    out = lax.map(chunk, (ht.reshape(nc, PEER_CHUNK, D), idx.reshape(nc, PEER_CHUNK, PEER_HEADS, PEER_TOPK), gate.reshape(nc, PEER_CHUNK, PEER_HEADS, PEER_TOPK)))
    return out.reshape(B_, S_, D).astype(h.dtype)


def _normal(key, shape, scale):
    return jax.random.normal(key, shape, jnp.float32) * scale


def setup_inputs(seed: int = 0) -> dict:
    key = jax.random.key(seed)
    ks = jax.random.split(key, 32)
    L, D = DEPTH, D_MODEL
    lam_u = jax.random.uniform(ks[10], (L, A_WIDTH), jnp.float32, 0.9, 0.999)
    lam_a = lam_u ** (1.0 / A_C)
    return {
        'x': _normal(ks[0], (BATCH, SEQ, D), 1.0),
        'mem': _normal(ks[1], (BATCH, MEM_LEN, D), 1.0),
        'norm_mix': 1.0 + _normal(ks[2], (L, D), 0.02),
        'w_in': _normal(ks[3], (L, D, IN_WIDTH), D ** -0.5),
        'conv_w': _normal(ks[4], (L, A_CONV_WIDTH, A_WIDTH), A_CONV_WIDTH ** -0.5),
        'conv_b': _normal(ks[5], (L, A_WIDTH), 0.01),
        'lru_wr': _normal(ks[6], (L, A_HEADS, HEAD_DIM, HEAD_DIM), HEAD_DIM ** -0.5),
        'lru_br': _normal(ks[7], (L, A_WIDTH), 0.01),
        'lru_wi': _normal(ks[8], (L, A_HEADS, HEAD_DIM, HEAD_DIM), HEAD_DIM ** -0.5),
        'lru_bi': _normal(ks[9], (L, A_WIDTH), 0.01),
        'lru_lambda': jnp.log(lam_a) - jnp.log1p(-lam_a),
        'gla_w_lr': _normal(ks[11], (L, C_LOW_RANK, C_WIDTH), C_LOW_RANK ** -0.5),
        'gla_b_gate': _normal(ks[12], (L, C_WIDTH), 0.01),
        'gla_norm': 1.0 + _normal(ks[13], (L, C_DV), 0.02),
        'w_pa': _normal(ks[14], (L, A_WIDTH, D), A_WIDTH ** -0.5),
        'w_pb': _normal(ks[15], (L, B_WIDTH, D), B_WIDTH ** -0.5),
        'w_pc': _normal(ks[16], (L, C_WIDTH, D), C_WIDTH ** -0.5),
        'w_out': _normal(ks[17], (L, D, D), D ** -0.5),
        'norm_cross': 1.0 + _normal(ks[18], (L, D), 0.02),
        'norm_mem': 1.0 + _normal(ks[19], (L, D), 0.02),
        'x_wq': _normal(ks[20], (L, D, X_WIDTH), D ** -0.5),
        'x_wk': _normal(ks[21], (L, D, X_WIDTH), D ** -0.5),
        'x_wv': _normal(ks[22], (L, D, X_WIDTH), D ** -0.5),
        'x_wo': _normal(ks[23], (L, X_WIDTH, D), X_WIDTH ** -0.5),
        'norm_ffn': 1.0 + _normal(ks[24], (L, D), 0.02),
        'peer_wq': _normal(ks[25], (L, D, PEER_HEADS * PEER_QDIM), D ** -0.5),
        'peer_keys': _normal(ks[26], (L, PEER_HEADS, 2, PEER_NKEYS, PEER_QDIM // 2), (PEER_QDIM // 2) ** -0.5),
        'peer_down': _normal(ks[27], (L, PEER_N, D), D ** -0.5),
        'peer_up': _normal(ks[28], (L, PEER_N, D), PEER_HEADS ** -0.5),
        'norm_final': 1.0 + _normal(ks[29], (D,), 0.02),
    }


def reference(x, mem, norm_mix, w_in, conv_w, conv_b, lru_wr, lru_br, lru_wi, lru_bi, lru_lambda, gla_w_lr, gla_b_gate, gla_norm, w_pa, w_pb, w_pc, w_out, norm_cross, norm_mem, x_wq, x_wk, x_wv, x_wo, norm_ffn, peer_wq, peer_keys, peer_down, peer_up, norm_final):
    for l in range(DEPTH):
        h = rms_norm(x, norm_mix[l])
        x = x + hybrid_mixer(h, w_in[l], conv_w[l], conv_b[l], lru_wr[l], lru_br[l], lru_wi[l], lru_bi[l], lru_lambda[l], gla_w_lr[l], gla_b_gate[l], gla_norm[l], w_pa[l], w_pb[l], w_pc[l], w_out[l])
        h = rms_norm(x, norm_cross[l])
        m = rms_norm(mem, norm_mem[l])
        x = x + memory_cross_attention(h, m, x_wq[l], x_wk[l], x_wv[l], x_wo[l])
        h = rms_norm(x, norm_ffn[l])
        x = x + peer_ffn(h, peer_wq[l], peer_keys[l], peer_down[l], peer_up[l])
    return rms_norm(x, norm_final)
```

```python
import functools
import math

import jax
import jax.numpy as jnp
from jax import lax
from jax.experimental import pallas as pl
from jax.experimental.pallas import tpu as pltpu

F32 = jnp.float32
BF16 = jnp.bfloat16

LANES = 128
SUBLANES = 8
VMEM_LIMIT_BYTES = 56 * 1024 * 1024

EPS = 1e-6
HEAD_DIM = 128
LRU_C = 8.0
GLA_CHUNK = 64
GLA_GATE_TEMP = 16.0
X_HEADS = 4
PEER_HEADS = 8
PEER_NKEYS = 128
PEER_TOPK = 16
EXP_UNDERFLOW = -105.0
NOT_RANKED = 4096.0

NT_DIMS = (((1,), (1,)), ((), ()))
TN_DIMS = (((0,), (0,)), ((), ()))


def _params(*semantics):
    return pltpu.CompilerParams(dimension_semantics=semantics, vmem_limit_bytes=VMEM_LIMIT_BYTES)


def _gelu_tanh(x):
    return 0.5 * x * (1.0 + jnp.tanh(math.sqrt(2.0 / math.pi) * (x + 0.044715 * (x * x * x))))


def _softplus(x):
    return jnp.maximum(x, 0.0) + jnp.log1p(jnp.exp(-jnp.abs(x)))


def _split_bf16(x):
    hi = x.astype(BF16)
    lo = (x - hi.astype(F32)).astype(BF16)
    return hi, lo


def _resnorm_kernel(*refs, n_delta, emit_x):
    x_ref = refs[0]
    d_refs = refs[1:1 + n_delta]
    g_ref = refs[1 + n_delta]
    outs = refs[2 + n_delta:]
    x = x_ref[...]
    for d in d_refs:
        x = x + d[...].astype(F32)
    y = x * lax.rsqrt(jnp.mean(x * x, axis=-1, keepdims=True) + EPS) * g_ref[...]
    if emit_x:
        outs[0][...] = x
        outs[1][...] = y.astype(outs[1].dtype)
    else:
        outs[0][...] = y.astype(outs[0].dtype)


def resnorm(x, deltas, g, *, emit_x, out_dtype, block_rows=256):
    rows, d = x.shape
    block_rows = min(block_rows, rows)
    row_spec = pl.BlockSpec((block_rows, d), lambda i: (i, 0))
    out_shape = [jax.ShapeDtypeStruct((rows, d), out_dtype)]
    out_specs = [row_spec]
    if emit_x:
        out_shape = [jax.ShapeDtypeStruct((rows, d), F32)] + out_shape
        out_specs = [row_spec] + out_specs
    res = pl.pallas_call(
        functools.partial(_resnorm_kernel, n_delta=len(deltas), emit_x=emit_x),
        grid=(rows // block_rows,),
        in_specs=[row_spec] * (1 + len(deltas)) + [pl.BlockSpec((1, d), lambda i: (0, 0))],
        out_specs=out_specs,
        out_shape=out_shape,
        compiler_params=_params("arbitrary"),
        name="resnorm",
    )(x, *deltas, g.reshape(1, d).astype(F32))
    return res if emit_x else res[0]


def _mm_kernel(a_ref, w_ref, o_ref):
    o_ref[...] = jnp.dot(a_ref[...], w_ref[...], preferred_element_type=F32).astype(o_ref.dtype)


def matmul(a, w, out_dtype, *, bm=1024, bn=1024):
    m, k = a.shape
    n = w.shape[1]
    bm, bn = min(bm, m), min(bn, n)
    return pl.pallas_call(
        _mm_kernel,
        grid=(m // bm, n // bn),
        in_specs=[pl.BlockSpec((bm, k), lambda i, j: (i, 0)),
                  pl.BlockSpec((k, bn), lambda i, j: (0, j))],
        out_specs=pl.BlockSpec((bm, bn), lambda i, j: (i, j)),
        out_shape=jax.ShapeDtypeStruct((m, n), out_dtype),
        compiler_params=_params("arbitrary", "arbitrary"),
        name="matmul",
    )(a, w)


def _lru_kernel(ax_ref, gate_ref, cw_ref, cb_ref, wr_ref, br_ref, wi_ref, bi_ref, lam_ref,
                o_ref, tail_ref, hlast_ref, a_scr, u_scr, *, heads_per_block, conv_width):
    t = pl.program_id(1)
    tb, cb = ax_ref.shape

    @pl.when(t == 0)
    def _():
        tail_ref[...] = jnp.zeros_like(tail_ref)
        hlast_ref[...] = jnp.zeros_like(hlast_ref)

    x = ax_ref[...].astype(F32)
    prev = tail_ref[...]
    row8 = lax.broadcasted_iota(jnp.int32, (SUBLANES, cb), 0)
    cw = cw_ref[...]
    y = x * cw[conv_width - 1:conv_width, :] + cb_ref[...]
    for d in range(1, conv_width):
        rolled = pltpu.roll(x, d, 0)
        first = jnp.where(row8 < d, pltpu.roll(prev, d, 0), rolled[:SUBLANES])
        shifted = jnp.concatenate([first, rolled[SUBLANES:]], axis=0)
        y = y + shifted * cw[conv_width - 1 - d:conv_width - d, :]
    tail_ref[...] = x[tb - SUBLANES:, :]

    yb = y.astype(BF16)
    r_parts, i_parts = [], []
    for hh in range(heads_per_block):
        yh = yb[:, hh * HEAD_DIM:(hh + 1) * HEAD_DIM]
        r_parts.append(jnp.dot(yh, wr_ref[hh], preferred_element_type=F32))
        i_parts.append(jnp.dot(yh, wi_ref[hh], preferred_element_type=F32))
    r = jax.nn.sigmoid(jnp.concatenate(r_parts, axis=1) + br_ref[...])
    i = jax.nn.sigmoid(jnp.concatenate(i_parts, axis=1) + bi_ref[...])
    log_a = (-LRU_C) * r * _softplus(-lam_ref[...])
    a = jnp.exp(log_a)
    a_scr[...] = a
    u_scr[...] = jnp.sqrt(-jnp.tanh(log_a) * (a * a + 1.0)) * (i * y)

    def group(gidx, h_prev):
        rows = pl.ds(pl.multiple_of(gidx * SUBLANES, SUBLANES), SUBLANES)
        a = a_scr[rows, :]
        u = u_scr[rows, :]
        for d in (1, 2, 4):
            keep = row8 >= d
            u = jnp.where(keep, a * pltpu.roll(u, d, 0) + u, u)
            a = jnp.where(keep, a * pltpu.roll(a, d, 0), a)
        h = a * h_prev + u
        u_scr[rows, :] = h
        return h[SUBLANES - 1:SUBLANES, :]

    h_last = lax.fori_loop(0, tb // SUBLANES, group, hlast_ref[0:1, :])
    hlast_ref[0:1, :] = h_last
    o_ref[...] = (u_scr[...] * _gelu_tanh(gate_ref[...].astype(F32))).astype(o_ref.dtype)


def lru_branch(proj, conv_w, conv_b, wr, br, wi, bi, lam, *, width, tb=512, cb=256):
    s = proj.shape[0]
    tb = min(tb, s)
    nc = width // cb
    hpb = cb // HEAD_DIM
    kw = conv_w.shape[0]
    vec = lambda a: a.reshape(1, width).astype(F32)
    vec_spec = pl.BlockSpec((1, cb), lambda c, t: (0, c))
    return pl.pallas_call(
        functools.partial(_lru_kernel, heads_per_block=hpb, conv_width=kw),
        grid=(nc, s // tb),
        in_specs=[pl.BlockSpec((tb, cb), lambda c, t: (t, c)),
                  pl.BlockSpec((tb, cb), lambda c, t: (t, nc + c)),
                  pl.BlockSpec((kw, cb), lambda c, t: (0, c)),
                  vec_spec,
                  pl.BlockSpec((hpb, HEAD_DIM, HEAD_DIM), lambda c, t: (c, 0, 0)),
                  vec_spec,
                  pl.BlockSpec((hpb, HEAD_DIM, HEAD_DIM), lambda c, t: (c, 0, 0)),
                  vec_spec, vec_spec],
        out_specs=pl.BlockSpec((tb, cb), lambda c, t: (t, c)),
        out_shape=jax.ShapeDtypeStruct((s, width), BF16),
        scratch_shapes=[pltpu.VMEM((SUBLANES, cb), F32), pltpu.VMEM((SUBLANES, cb), F32),
                        pltpu.VMEM((tb, cb), F32), pltpu.VMEM((tb, cb), F32)],
        compiler_params=_params("arbitrary", "arbitrary"),
        name="lru_branch",
    )(proj, proj, conv_w.astype(F32), vec(conv_b), wr.astype(BF16), vec(br), wi.astype(BF16),
      vec(bi), vec(lam))


def _stick_kernel(q_ref, k_ref, v_ref, o_ref, c_scr, acc_scr, *, tk):
    qi = pl.program_id(1)
    tq = q_ref.shape[0]
    q = q_ref[...]
    scale = HEAD_DIM ** -0.5
    rr = lax.broadcasted_iota(jnp.int32, (tk, 2 * tk), 0)
    cc = lax.broadcasted_iota(jnp.int32, (tk, 2 * tk), 1)
    cum_mat = jnp.where((rr > cc) | (cc >= tk), 1.0, 0.0).astype(BF16)

    def block(j, c, causal_mask):
        rows = pl.ds(pl.multiple_of(j * tk, tk), tk)
        k = k_ref[rows, :]
        v = v_ref[rows, :]
        z = lax.dot_general(q, k, NT_DIMS, preferred_element_type=F32) * scale
        sp = _softplus(z)
        log_keep = -sp
        log_beta = z - sp
        if causal_mask is not None:
            log_keep = jnp.where(causal_mask, log_keep, 0.0)
        hi, lo = _split_bf16(log_keep)
        cum = (jnp.dot(hi, cum_mat, preferred_element_type=F32)
               + jnp.dot(lo, cum_mat, preferred_element_type=F32))
        w = jnp.exp(log_beta + cum[:, :tk] + c)
        if causal_mask is not None:
            w = jnp.where(causal_mask, w, 0.0)
        pv = jnp.dot(w.astype(BF16), v, preferred_element_type=F32)
        return pv, c + cum[:, tk:]

    qpos = lax.broadcasted_iota(jnp.int32, (tq, tk), 0)
    kpos = lax.broadcasted_iota(jnp.int32, (tq, tk), 1)
    pv, c = block(qi, jnp.zeros((tq, tk), F32), kpos < qpos)
    acc_scr[...] = pv
    c_scr[...] = c

    def cond(carry):
        j, cmax = carry
        return jnp.logical_and(j >= 0, cmax > EXP_UNDERFLOW)

    def body(carry):
        j, _ = carry
        pv, c = block(j, c_scr[...], None)
        acc_scr[...] += pv
        c_scr[...] = c
        return j - 1, jnp.max(c)

    lax.while_loop(cond, body, (qi - 1, jnp.max(c)))
    o_ref[...] = acc_scr[...].astype(o_ref.dtype)


def stick_breaking(proj, *, heads, q_col, k_col, v_col, tq=128):
    s = proj.shape[0]
    tq = min(tq, s)
    return pl.pallas_call(
        functools.partial(_stick_kernel, tk=tq),
        grid=(heads, s // tq),
        in_specs=[pl.BlockSpec((tq, HEAD_DIM), lambda h, i: (i, q_col + h)),
                  pl.BlockSpec((s, HEAD_DIM), lambda h, i: (0, k_col + h)),
                  pl.BlockSpec((s, HEAD_DIM), lambda h, i: (0, v_col + h))],
        out_specs=pl.BlockSpec((tq, HEAD_DIM), lambda h, i: (i, h)),
        out_shape=jax.ShapeDtypeStruct((s, heads * HEAD_DIM), BF16),
        scratch_shapes=[pltpu.VMEM((tq, tq), F32), pltpu.VMEM((tq, HEAD_DIM), F32)],
        compiler_params=_params("arbitrary", "arbitrary"),
        name="stick_breaking",
    )(proj, proj, proj)


def _gla_kernel(q_ref, k_ref, v_ref, gate_ref, lr_ref, wlr_ref, bg_ref, gn_ref, o_ref, state_scr):
    t = pl.program_id(1)
    tc = q_ref.shape[0]

    @pl.when(t == 0)
    def _():
        state_scr[...] = jnp.zeros_like(state_scr)

    pre = jnp.dot(lr_ref[...], wlr_ref[...], preferred_element_type=F32) + bg_ref[...]
    log_g = -_softplus(-pre) * (1.0 / GLA_GATE_TEMP)

    rr = lax.broadcasted_iota(jnp.int32, (tc, tc), 0)
    cc = lax.broadcasted_iota(jnp.int32, (tc, tc), 1)
    same = (rr // GLA_CHUNK) == (cc // GLA_CHUNK)
    cum_mat = jnp.where(same & (cc <= rr), 1.0, 0.0).astype(BF16)
    tot_mat = jnp.where(same, 1.0, 0.0).astype(BF16)
    hi, lo = _split_bf16(log_g)
    b = jnp.dot(cum_mat, hi, preferred_element_type=F32) + jnp.dot(cum_mat, lo, preferred_element_type=F32)
    b_last = jnp.dot(tot_mat, hi, preferred_element_type=F32) + jnp.dot(tot_mat, lo, preferred_element_type=F32)

    q = q_ref[...].astype(F32) * (HEAD_DIM ** -0.5)
    k = k_ref[...].astype(F32)
    q_dec = (q * jnp.exp(b)).astype(BF16)
    k_in = (k * jnp.exp(-b)).astype(BF16)
    k_state = (k * jnp.exp(b_last - b)).astype(BF16)
    v = v_ref[...]
    tril = (lax.broadcasted_iota(jnp.int32, (GLA_CHUNK, GLA_CHUNK), 1)
            <= lax.broadcasted_iota(jnp.int32, (GLA_CHUNK, GLA_CHUNK), 0))

    state_t = state_scr[...]
    outs = []
    for ci in range(tc // GLA_CHUNK):
        sl = slice(ci * GLA_CHUNK, (ci + 1) * GLA_CHUNK)
        p = lax.dot_general(q_dec[sl], k_in[sl], NT_DIMS, preferred_element_type=F32)
        p = jnp.where(tril, p, 0.0)
        o = jnp.dot(p.astype(BF16), v[sl], preferred_element_type=F32)
        o = o + lax.dot_general(q_dec[sl], state_t.astype(BF16), NT_DIMS, preferred_element_type=F32)
        decay = jnp.exp(b_last[ci * GLA_CHUNK:ci * GLA_CHUNK + 1, :])
        state_t = state_t * decay + lax.dot_general(v[sl], k_state[sl], TN_DIMS,
                                                    preferred_element_type=F32)
        outs.append(o)
    state_scr[...] = state_t
    o = jnp.concatenate(outs, axis=0)
    o = o * lax.rsqrt(jnp.mean(o * o, axis=-1, keepdims=True) + EPS) * gn_ref[...]
    g = gate_ref[...].astype(F32)
    o_ref[...] = (o * (g * jax.nn.sigmoid(g))).astype(o_ref.dtype)


def gla_branch(proj, lr, w_lr, b_gate, gn, *, heads, q_col, k_col, v_col, gate_col, tc=256):
    s = proj.shape[0]
    tc = min(tc, s)
    blk = lambda col: pl.BlockSpec((tc, HEAD_DIM), lambda h, t: (t, col + h))
    return pl.pallas_call(
        _gla_kernel,
        grid=(heads, s // tc),
        in_specs=[blk(q_col), blk(k_col), blk(v_col), blk(gate_col),
                  pl.BlockSpec((tc, LANES), lambda h, t: (t, 0)),
                  pl.BlockSpec((LANES, HEAD_DIM), lambda h, t: (0, h)),
                  pl.BlockSpec((1, HEAD_DIM), lambda h, t: (0, h)),
                  pl.BlockSpec((1, HEAD_DIM), lambda h, t: (0, 0))],
        out_specs=pl.BlockSpec((tc, HEAD_DIM), lambda h, t: (t, h)),
        out_shape=jax.ShapeDtypeStruct((s, heads * HEAD_DIM), BF16),
        scratch_shapes=[pltpu.VMEM((HEAD_DIM, HEAD_DIM), F32)],
        compiler_params=_params("arbitrary", "arbitrary"),
        name="gla_branch",
    )(proj, proj, proj, proj, lr, w_lr, b_gate.reshape(1, -1).astype(F32), gn.reshape(1, -1).astype(F32))


def _merge_kernel(ya_ref, yb_ref, yc_ref, wa_ref, wb_ref, wc_ref, ga_ref, gb_ref, gc_ref, o_ref):
    def term(y_ref, w_ref, g_ref):
        return (jax.nn.sigmoid(g_ref[...].astype(F32))
                * jnp.dot(y_ref[...], w_ref[...], preferred_element_type=F32))
    o_ref[...] = (term(ya_ref, wa_ref, ga_ref) + term(yb_ref, wb_ref, gb_ref)
                  + term(yc_ref, wc_ref, gc_ref)).astype(o_ref.dtype)


def merge_branches(ya, yb, yc, wa, wb, wc, gates, *, bm=512, bn=1024):
    s, w = ya.shape
    d = wa.shape[1]
    bm, bn = min(bm, s), min(bn, d)
    nb = d // bn
    y_spec = pl.BlockSpec((bm, w), lambda i, j: (i, 0))
    w_spec = pl.BlockSpec((w, bn), lambda i, j: (0, j))
    g_spec = lambda b: pl.BlockSpec((bm, bn), lambda i, j: (i, b * nb + j))
    return pl.pallas_call(
        _merge_kernel,
        grid=(s // bm, nb),
        in_specs=[y_spec, y_spec, y_spec, w_spec, w_spec, w_spec, g_spec(0), g_spec(1), g_spec(2)],
        out_specs=pl.BlockSpec((bm, bn), lambda i, j: (i, j)),
        out_shape=jax.ShapeDtypeStruct((s, d), BF16),
        compiler_params=_params("arbitrary", "arbitrary"),
        name="merge_branches",
    )(ya, yb, yc, wa, wb, wc, gates, gates, gates)


def _xattn_kernel(q_ref, k_ref, v_ref, o_ref, *, heads):
    hd = q_ref.shape[1] // heads
    scale = hd ** -0.5
    outs = []
    for h in range(heads):
        sl = slice(h * hd, (h + 1) * hd)
        s = lax.dot_general(q_ref[:, sl], k_ref[:, sl], NT_DIMS, preferred_element_type=F32) * scale
        s = s - jnp.max(s, axis=-1, keepdims=True)
        e = jnp.exp(s)
        p = e / jnp.sum(e, axis=-1, keepdims=True)
        outs.append(jnp.dot(p.astype(BF16), v_ref[:, sl], preferred_element_type=F32))
    o_ref[...] = jnp.concatenate(outs, axis=1).astype(o_ref.dtype)


def cross_attention(q, k, v, *, heads, tq=512):
    s, w = q.shape
    m = k.shape[0]
    tq = min(tq, s)
    return pl.pallas_call(
        functools.partial(_xattn_kernel, heads=heads),
        grid=(s // tq,),
        in_specs=[pl.BlockSpec((tq, w), lambda i: (i, 0)),
                  pl.BlockSpec((m, w), lambda i: (0, 0)),
                  pl.BlockSpec((m, w), lambda i: (0, 0))],
        out_specs=pl.BlockSpec((tq, w), lambda i: (i, 0)),
        out_shape=jax.ShapeDtypeStruct((s, w), BF16),
        compiler_params=_params("arbitrary"),
        name="cross_attention",
    )(q, k, v)


def _topk_rows(s, k):
    n = s.shape[0]
    rows = lax.broadcasted_iota(jnp.int32, s.shape, 0).astype(F32)
    rank = jnp.full(s.shape, NOT_RANKED, F32)
    vals, idxs = [], []
    for it in range(k):
        m = jnp.max(s, axis=0, keepdims=True)
        idx = jnp.min(jnp.where(s == m, rows, float(n)), axis=0, keepdims=True)
        hit = rows == idx
        rank = jnp.where(hit, float(it), rank)
        s = jnp.where(hit, -jnp.inf, s)
        vals.append(m)
        idxs.append(idx)
    return vals, idxs, rank


def _peer_select_kernel(q_ref, keys_ref, s0_ref, s1_ref, e0_ref, e1_ref, r0_ref, r1_ref,
                        tau_ref, pt_ref):
    nk = keys_ref.shape[2]
    qd = keys_ref.shape[3]
    part = []
    for p in range(2):
        qh, ql = _split_bf16(q_ref[:, p * qd:(p + 1) * qd])
        kh, kl = _split_bf16(keys_ref[0, p])
        s = (lax.dot_general(kh, qh, NT_DIMS, preferred_element_type=F32)
             + lax.dot_general(kh, ql, NT_DIMS, preferred_element_type=F32)
             + lax.dot_general(kl, qh, NT_DIMS, preferred_element_type=F32))
        vals, _, rank = _topk_rows(s, PEER_TOPK)
        part.append((s, vals, rank))
    (s0, v0, rank0), (s1, v1, rank1) = part
    v1_all = jnp.concatenate(v1, axis=0)
    cand = jnp.concatenate([v0[a] + v1_all for a in range(PEER_TOPK)], axis=0)
    best, pos, _ = _topk_rows(cand, PEER_TOPK)
    z = best[0] - best[0]
    for b in best:
        z = z + jnp.exp(b - best[0])
    in0 = rank0 < NOT_RANKED
    in1 = rank1 < NOT_RANKED
    s0m = jnp.where(in0, s0, -jnp.inf)
    s1m = jnp.where(in1, s1, -jnp.inf)
    s0_ref[...] = s0m
    s1_ref[...] = s1m
    e0_ref[...] = jnp.exp(s0m - v0[0])
    e1_ref[...] = jnp.exp(s1m - v1[0]) / z
    r0_ref[...] = rank0 * float(PEER_TOPK)
    r1_ref[...] = rank1
    tau_ref[0] = best[PEER_TOPK - 1]
    pt_ref[0] = pos[PEER_TOPK - 1]


def peer_select(q, keys, *, tb=512):
    t = q.shape[0]
    h, _, nk, qd = keys.shape
    tb = min(tb, t)
    tab = jax.ShapeDtypeStruct((h * nk, t), F32)
    row = jax.ShapeDtypeStruct((h, 1, t), F32)
    tab_spec = pl.BlockSpec((nk, tb), lambda i, hh: (hh, i))
    row_spec = pl.BlockSpec((1, 1, tb), lambda i, hh: (hh, 0, i))
    return pl.pallas_call(
        _peer_select_kernel,
        grid=(t // tb, h),
        in_specs=[pl.BlockSpec((tb, 2 * qd), lambda i, hh: (i, hh)),
                  pl.BlockSpec((1, 2, nk, qd), lambda i, hh: (hh, 0, 0, 0))],
        out_specs=[tab_spec] * 6 + [row_spec] * 2,
        out_shape=[tab] * 6 + [row] * 2,
        compiler_params=_params("arbitrary", "arbitrary"),
        name="peer_select",
    )(q, keys)


def _peer_expert_kernel(h_ref, down_ref, up_ref, s0_ref, s1_ref, e0_ref, e1_ref, r0_ref, r1_ref,
                        tau_ref, pt_ref, o_ref, *, heads, nk):
    e = pl.program_id(1)
    eb = down_ref.shape[0]

    @pl.when(e == 0)
    def _():
        o_ref[...] = jnp.zeros_like(o_ref)

    act = lax.dot_general(down_ref[...], h_ref[...], NT_DIMS, preferred_element_type=F32)
    act = _gelu_tanh(act)
    w_rows = []
    for il in range(eb // nk):
        i = e * (eb // nk) + il
        w = None
        for h in range(heads):
            row = pl.ds(h * nk + i, 1)
            blk = slice(h * nk, (h + 1) * nk)
            score = s0_ref[row, :] + s1_ref[blk, :]
            order = r0_ref[row, :] + r1_ref[blk, :]
            tau = tau_ref[h]
            sel = (score > tau) | ((score == tau) & (order <= pt_ref[h]))
            contrib = jnp.where(sel, e0_ref[row, :] * e1_ref[blk, :], 0.0)
            w = contrib if w is None else w + contrib
        w_rows.append(w)
    wt = jnp.concatenate(w_rows, axis=0)
    xt = (wt * act).astype(BF16)
    o_ref[...] += lax.dot_general(xt, up_ref[...], TN_DIMS, preferred_element_type=F32)


def peer_experts(h, down, up, tables, *, heads, nk, tb=512, eb=512):
    t, d = h.shape
    n_exp = down.shape[0]
    tb, eb = min(tb, t), min(eb, n_exp)
    once = dict(pipeline_mode=pl.Buffered(1))
    tab_spec = pl.BlockSpec((heads * nk, tb), lambda i, e: (0, i), **once)
    row_spec = pl.BlockSpec((heads, 1, tb), lambda i, e: (0, 0, i), **once)
    return pl.pallas_call(
        functools.partial(_peer_expert_kernel, heads=heads, nk=nk),
        grid=(t // tb, n_exp // eb),
        in_specs=[pl.BlockSpec((tb, d), lambda i, e: (i, 0), **once),
                  pl.BlockSpec((eb, d), lambda i, e: (e, 0)),
                  pl.BlockSpec((eb, d), lambda i, e: (e, 0))] + [tab_spec] * 6 + [row_spec] * 2,
        out_specs=pl.BlockSpec((tb, d), lambda i, e: (i, 0)),
        out_shape=jax.ShapeDtypeStruct((t, d), F32),
        compiler_params=_params("arbitrary", "arbitrary"),
        name="peer_experts",
    )(h, down, up, *tables)


def kernel(x, mem, norm_mix, w_in, conv_w, conv_b, lru_wr, lru_br, lru_wi, lru_bi, lru_lambda, gla_w_lr, gla_b_gate, gla_norm, w_pa, w_pb, w_pc, w_out, norm_cross, norm_mem, x_wq, x_wk, x_wv, x_wo, norm_ffn, peer_wq, peer_keys, peer_down, peer_up, norm_final):
    depth = w_in.shape[0]
    batch, seq, d = x.shape
    assert batch == 1
    width = conv_w.shape[2]
    heads = width // HEAD_DIM
    rank = gla_w_lr.shape[1]
    n_main = 9 * width
    col = lambda k: k * heads

    xs = x.reshape(seq, d)
    mem2 = mem.reshape(mem.shape[1], d)
    deltas = []
    for l in range(depth):
        w_main = w_in[l, :, :n_main].astype(BF16)
        w_lr = jnp.pad(w_in[l, :, n_main:n_main + rank], ((0, 0), (0, LANES - rank))).astype(BF16)
        w_gates = w_in[l, :, n_main + rank:].astype(BF16)

        if l == 0:
            h = resnorm(xs, [], norm_mix[l], emit_x=False, out_dtype=BF16)
        else:
            xs, h = resnorm(xs, deltas, norm_mix[l], emit_x=True, out_dtype=BF16)
        proj = matmul(h, w_main, BF16)
        lr = matmul(h, w_lr, BF16, bn=LANES)
        gates = matmul(h, w_gates, BF16)

        ya = lru_branch(proj, conv_w[l], conv_b[l], lru_wr[l], lru_br[l], lru_wi[l], lru_bi[l],
                        lru_lambda[l], width=width)
        yb = stick_breaking(proj, heads=heads, q_col=col(2), k_col=col(3), v_col=col(4))
        w_lr_gate = jnp.pad(gla_w_lr[l], ((0, LANES - rank), (0, 0))).astype(BF16)
        yc = gla_branch(proj, lr, w_lr_gate, gla_b_gate[l], gla_norm[l], heads=heads,
                        q_col=col(5), k_col=col(6), v_col=col(7), gate_col=col(8))
        mixed = merge_branches(ya, yb, yc, w_pa[l].astype(BF16), w_pb[l].astype(BF16),
                               w_pc[l].astype(BF16), gates)
        delta = matmul(mixed, w_out[l].astype(BF16), F32)

        xs, h = resnorm(xs, [delta], norm_cross[l], emit_x=True, out_dtype=BF16)
        m = resnorm(mem2, [], norm_mem[l], emit_x=False, out_dtype=BF16)
        q = matmul(h, x_wq[l].astype(BF16), BF16)
        k = matmul(m, x_wk[l].astype(BF16), BF16)
        v = matmul(m, x_wv[l].astype(BF16), BF16)
        o = cross_attention(q, k, v, heads=X_HEADS)
        delta = matmul(o, x_wo[l].astype(BF16), F32)

        xs, h = resnorm(xs, [delta], norm_ffn[l], emit_x=True, out_dtype=BF16)
        pq = matmul(h, peer_wq[l].astype(BF16), F32)
        tables = peer_select(pq, peer_keys[l])
        delta = peer_experts(h, peer_down[l].astype(BF16), peer_up[l].astype(BF16), tables,
                             heads=PEER_HEADS, nk=PEER_NKEYS)
        deltas = [delta]

    out = resnorm(xs, deltas, norm_final, emit_x=False, out_dtype=F32)
    return out.reshape(batch, seq, d)
```

```python
import functools
import math

import jax
import jax.numpy as jnp
from jax import lax
from jax.experimental import pallas as pl
from jax.experimental.pallas import tpu as pltpu

F32 = jnp.float32
BF16 = jnp.bfloat16

LANES = 128
SUBLANES = 8
VMEM_LIMIT_BYTES = 56 * 1024 * 1024

EPS = 1e-6
HEAD_DIM = 128
LRU_C = 8.0
GLA_CHUNK = 64
GLA_GATE_TEMP = 16.0
X_HEADS = 4
PEER_TOPK = 16
EXP_UNDERFLOW = -105.0
NOT_RANKED = 4096.0

NT_DIMS = (((1,), (1,)), ((), ()))
TN_DIMS = (((0,), (0,)), ((), ()))


def _params(*semantics, flags=None):
    return pltpu.CompilerParams(dimension_semantics=semantics, vmem_limit_bytes=VMEM_LIMIT_BYTES,
                                flags=flags)


def _gelu_tanh(x):
    return 0.5 * x * (1.0 + jnp.tanh(math.sqrt(2.0 / math.pi) * (x + 0.044715 * (x * x * x))))


def _softplus(x):
    return jnp.maximum(x, 0.0) + jnp.log1p(jnp.exp(-jnp.abs(x)))


def _split_bf16(x):
    hi = x.astype(BF16)
    lo = (x - hi.astype(F32)).astype(BF16)
    return hi, lo


def _resnorm_kernel(*refs, n_delta, emit_x):
    x_ref = refs[0]
    d_refs = refs[1:1 + n_delta]
    g_ref = refs[1 + n_delta]
    outs = refs[2 + n_delta:]
    x = x_ref[...]
    for d in d_refs:
        x = x + d[...].astype(F32)
    y = x * lax.rsqrt(jnp.mean(x * x, axis=-1, keepdims=True) + EPS) * g_ref[...]
    if emit_x:
        outs[0][...] = x
        outs[1][...] = y.astype(outs[1].dtype)
    else:
        outs[0][...] = y.astype(outs[0].dtype)


def resnorm(x, deltas, g, *, emit_x, out_dtype, block_rows=256):
    rows, d = x.shape
    block_rows = min(block_rows, rows)
    row_spec = pl.BlockSpec((block_rows, d), lambda i: (i, 0))
    out_shape = [jax.ShapeDtypeStruct((rows, d), out_dtype)]
    out_specs = [row_spec]
    if emit_x:
        out_shape = [jax.ShapeDtypeStruct((rows, d), F32)] + out_shape
        out_specs = [row_spec] + out_specs
    res = pl.pallas_call(
        functools.partial(_resnorm_kernel, n_delta=len(deltas), emit_x=emit_x),
        grid=(rows // block_rows,),
        in_specs=[row_spec] * (1 + len(deltas)) + [pl.BlockSpec((1, d), lambda i: (0, 0))],
        out_specs=out_specs,
        out_shape=out_shape,
        compiler_params=_params("arbitrary"),
        name="resnorm",
    )(x, *deltas, g.reshape(1, d).astype(F32))
    return res if emit_x else res[0]


def _mm_kernel(a_ref, w_ref, o_ref):
    o_ref[...] = jnp.dot(a_ref[...], w_ref[...], preferred_element_type=F32).astype(o_ref.dtype)


def matmul(a, w, out_dtype, *, bm=1024, bn=1024):
    m, k = a.shape
    n = w.shape[1]
    bm, bn = min(bm, m), min(bn, n)
    return pl.pallas_call(
        _mm_kernel,
        grid=(m // bm, n // bn),
        in_specs=[pl.BlockSpec((bm, k), lambda i, j: (i, 0)),
                  pl.BlockSpec((k, bn), lambda i, j: (0, j))],
        out_specs=pl.BlockSpec((bm, bn), lambda i, j: (i, j)),
        out_shape=jax.ShapeDtypeStruct((m, n), out_dtype),
        compiler_params=_params("arbitrary", "arbitrary"),
        name="matmul",
    )(a, w)


def _lru_kernel(ax_ref, gate_ref, cw_ref, cb_ref, wr_ref, br_ref, wi_ref, bi_ref, lam_ref,
                o_ref, tail_ref, hlast_ref, a_scr, u_scr, *, heads_per_block, conv_width):
    t = pl.program_id(1)
    tb, cb = ax_ref.shape

    @pl.when(t == 0)
    def _():
        tail_ref[...] = jnp.zeros_like(tail_ref)
        hlast_ref[...] = jnp.zeros_like(hlast_ref)

    x = ax_ref[...].astype(F32)
    prev = tail_ref[...]
    row8 = lax.broadcasted_iota(jnp.int32, (SUBLANES, cb), 0)
    cw = cw_ref[...]
    y = x * cw[conv_width - 1:conv_width, :] + cb_ref[...]
    for d in range(1, conv_width):
        rolled = pltpu.roll(x, d, 0)
        first = jnp.where(row8 < d, pltpu.roll(prev, d, 0), rolled[:SUBLANES])
        shifted = jnp.concatenate([first, rolled[SUBLANES:]], axis=0)
        y = y + shifted * cw[conv_width - 1 - d:conv_width - d, :]
    tail_ref[...] = x[tb - SUBLANES:, :]

    yb = y.astype(BF16)
    r_parts, i_parts = [], []
    for hh in range(heads_per_block):
        yh = yb[:, hh * HEAD_DIM:(hh + 1) * HEAD_DIM]
        r_parts.append(jnp.dot(yh, wr_ref[hh], preferred_element_type=F32))
        i_parts.append(jnp.dot(yh, wi_ref[hh], preferred_element_type=F32))
    r = jax.nn.sigmoid(jnp.concatenate(r_parts, axis=1) + br_ref[...])
    i = jax.nn.sigmoid(jnp.concatenate(i_parts, axis=1) + bi_ref[...])
    log_a = (-LRU_C) * r * _softplus(-lam_ref[...])
    a = jnp.exp(log_a)
    a_scr[...] = a
    u_scr[...] = jnp.sqrt(-jnp.tanh(log_a) * (a * a + 1.0)) * (i * y)

    def group(gidx, h_prev):
        rows = pl.ds(pl.multiple_of(gidx * SUBLANES, SUBLANES), SUBLANES)
        a = a_scr[rows, :]
        u = u_scr[rows, :]
        for d in (1, 2, 4):
            keep = row8 >= d
            u = jnp.where(keep, a * pltpu.roll(u, d, 0) + u, u)
            a = jnp.where(keep, a * pltpu.roll(a, d, 0), a)
        h = a * h_prev + u
        u_scr[rows, :] = h
        return h[SUBLANES - 1:SUBLANES, :]

    h_last = lax.fori_loop(0, tb // SUBLANES, group, hlast_ref[0:1, :])
    hlast_ref[0:1, :] = h_last
    o_ref[...] = (u_scr[...] * _gelu_tanh(gate_ref[...].astype(F32))).astype(o_ref.dtype)


def lru_branch(proj, conv_w, conv_b, wr, br, wi, bi, lam, *, width, tb=512, cb=256):
    s = proj.shape[0]
    tb = min(tb, s)
    nc = width // cb
    hpb = cb // HEAD_DIM
    kw = conv_w.shape[0]
    vec = lambda a: a.reshape(1, width).astype(F32)
    vec_spec = pl.BlockSpec((1, cb), lambda c, t: (0, c))
    return pl.pallas_call(
        functools.partial(_lru_kernel, heads_per_block=hpb, conv_width=kw),
        grid=(nc, s // tb),
        in_specs=[pl.BlockSpec((tb, cb), lambda c, t: (t, c)),
                  pl.BlockSpec((tb, cb), lambda c, t: (t, nc + c)),
                  pl.BlockSpec((kw, cb), lambda c, t: (0, c)),
                  vec_spec,
                  pl.BlockSpec((hpb, HEAD_DIM, HEAD_DIM), lambda c, t: (c, 0, 0)),
                  vec_spec,
                  pl.BlockSpec((hpb, HEAD_DIM, HEAD_DIM), lambda c, t: (c, 0, 0)),
                  vec_spec, vec_spec],
        out_specs=pl.BlockSpec((tb, cb), lambda c, t: (t, c)),
        out_shape=jax.ShapeDtypeStruct((s, width), BF16),
        scratch_shapes=[pltpu.VMEM((SUBLANES, cb), F32), pltpu.VMEM((SUBLANES, cb), F32),
                        pltpu.VMEM((tb, cb), F32), pltpu.VMEM((tb, cb), F32)],
        compiler_params=_params("arbitrary", "arbitrary"),
        name="lru_branch",
    )(proj, proj, conv_w.astype(F32), vec(conv_b), wr.astype(BF16), vec(br), wi.astype(BF16),
      vec(bi), vec(lam))


def _stick_kernel(q_ref, k_ref, v_ref, o_ref, c_scr, acc_scr, *, tk, group):
    qi = pl.program_id(1)
    tq = q_ref.shape[0]
    n_diag = tq // tk
    scale = HEAD_DIM ** -0.5
    rr = lax.broadcasted_iota(jnp.int32, (tk, 2 * tk), 0)
    cc = lax.broadcasted_iota(jnp.int32, (tk, 2 * tk), 1)
    cum_mat = jnp.where((rr > cc) | (cc >= tk), 1.0, 0.0).astype(BF16)

    def block(j, g, c, causal_mask):
        rows = pl.ds(pl.multiple_of(j * tk, tk), tk)
        lanes = slice(g * HEAD_DIM, (g + 1) * HEAD_DIM)
        z = lax.dot_general(q_ref[:, lanes], k_ref[rows, lanes], NT_DIMS,
                            preferred_element_type=F32) * scale
        sp = _softplus(z)
        log_keep = -sp
        log_beta = z - sp
        if causal_mask is not None:
            log_keep = jnp.where(causal_mask, log_keep, 0.0)
        hi, lo = _split_bf16(log_keep)
        cum = (jnp.dot(hi, cum_mat, preferred_element_type=F32)
               + jnp.dot(lo, cum_mat, preferred_element_type=F32))
        w = jnp.exp(log_beta + cum[:, :tk] + c)
        if causal_mask is not None:
            w = jnp.where(causal_mask, w, 0.0)
        pv = jnp.dot(w.astype(BF16), v_ref[rows, lanes], preferred_element_type=F32)
        return pv, c + cum[:, tk:]

    qrow = lax.broadcasted_iota(jnp.int32, (tq, tk), 0)
    kcol = lax.broadcasted_iota(jnp.int32, (tq, tk), 1)
    cs = [jnp.zeros((tq, tk), F32) for _ in range(group)]
    accs = [jnp.zeros((tq, HEAD_DIM), F32) for _ in range(group)]
    for dj in reversed(range(n_diag)):
        mask = (kcol + dj * tk) < qrow
        for g in range(group):
            pv, cs[g] = block(qi * n_diag + dj, g, cs[g], mask)
            accs[g] = accs[g] + pv
    cmax = jnp.max(cs[0])
    for g in range(group):
        c_scr[g] = cs[g]
        acc_scr[g] = accs[g]
        if g:
            cmax = jnp.maximum(cmax, jnp.max(cs[g]))

    def cond(carry):
        j, cmax = carry
        return jnp.logical_and(j >= 0, cmax > EXP_UNDERFLOW)

    def body(carry):
        j, _ = carry
        cmax = None
        for g in range(group):
            pv, c = block(j, g, c_scr[g], None)
            acc_scr[g] += pv
            c_scr[g] = c
            m = jnp.max(c)
            cmax = m if cmax is None else jnp.maximum(cmax, m)
        return j - 1, cmax

    lax.while_loop(cond, body, (qi * n_diag - 1, cmax))
    for g in range(group):
        o_ref[:, g * HEAD_DIM:(g + 1) * HEAD_DIM] = acc_scr[g].astype(o_ref.dtype)


def stick_breaking(proj, *, heads, q_col, k_col, v_col, tq=256, tk=128, group=4):
    s = proj.shape[0]
    tq = min(tq, s)
    gw = group * HEAD_DIM
    assert heads % group == 0 and q_col % group == 0 and k_col % group == 0 and v_col % group == 0
    return pl.pallas_call(
        functools.partial(_stick_kernel, tk=tk, group=group),
        grid=(heads // group, s // tq),
        in_specs=[pl.BlockSpec((tq, gw), lambda h, i: (i, q_col // group + h)),
                  pl.BlockSpec((s, gw), lambda h, i: (0, k_col // group + h)),
                  pl.BlockSpec((s, gw), lambda h, i: (0, v_col // group + h))],
        out_specs=pl.BlockSpec((tq, gw), lambda h, i: (i, h)),
        out_shape=jax.ShapeDtypeStruct((s, heads * HEAD_DIM), BF16),
        scratch_shapes=[pltpu.VMEM((group, tq, tk), F32), pltpu.VMEM((group, tq, HEAD_DIM), F32)],
        compiler_params=_params("arbitrary", "arbitrary"),
        name="stick_breaking",
    )(proj, proj, proj)


def _gla_kernel(q_ref, k_ref, v_ref, gate_ref, lr_ref, wlr_ref, bg_ref, gn_ref, o_ref, state_scr):
    t = pl.program_id(1)
    tc = q_ref.shape[0]

    @pl.when(t == 0)
    def _():
        state_scr[...] = jnp.zeros_like(state_scr)

    pre = jnp.dot(lr_ref[...], wlr_ref[...], preferred_element_type=F32) + bg_ref[...]
    log_g = -_softplus(-pre) * (1.0 / GLA_GATE_TEMP)

    rr = lax.broadcasted_iota(jnp.int32, (tc, tc), 0)
    cc = lax.broadcasted_iota(jnp.int32, (tc, tc), 1)
    same = (rr // GLA_CHUNK) == (cc // GLA_CHUNK)
    cum_mat = jnp.where(same & (cc <= rr), 1.0, 0.0).astype(BF16)
    tot_mat = jnp.where(same, 1.0, 0.0).astype(BF16)
    hi, lo = _split_bf16(log_g)
    b = jnp.dot(cum_mat, hi, preferred_element_type=F32) + jnp.dot(cum_mat, lo, preferred_element_type=F32)
    b_last = jnp.dot(tot_mat, hi, preferred_element_type=F32) + jnp.dot(tot_mat, lo, preferred_element_type=F32)

    q = q_ref[...].astype(F32) * (HEAD_DIM ** -0.5)
    k = k_ref[...].astype(F32)
    q_dec = (q * jnp.exp(b)).astype(BF16)
    k_in = (k * jnp.exp(-b)).astype(BF16)
    k_state = (k * jnp.exp(b_last - b)).astype(BF16)
    v = v_ref[...]
    tril = (lax.broadcasted_iota(jnp.int32, (GLA_CHUNK, GLA_CHUNK), 1)
            <= lax.broadcasted_iota(jnp.int32, (GLA_CHUNK, GLA_CHUNK), 0))

    state_t = state_scr[...]
    outs = []
    for ci in range(tc // GLA_CHUNK):
        sl = slice(ci * GLA_CHUNK, (ci + 1) * GLA_CHUNK)
        p = lax.dot_general(q_dec[sl], k_in[sl], NT_DIMS, preferred_element_type=F32)
        p = jnp.where(tril, p, 0.0)
        o = jnp.dot(p.astype(BF16), v[sl], preferred_element_type=F32)
        o = o + lax.dot_general(q_dec[sl], state_t.astype(BF16), NT_DIMS, preferred_element_type=F32)
        decay = jnp.exp(b_last[ci * GLA_CHUNK:ci * GLA_CHUNK + 1, :])
        state_t = state_t * decay + lax.dot_general(v[sl], k_state[sl], TN_DIMS,
                                                    preferred_element_type=F32)
        outs.append(o)
    state_scr[...] = state_t
    o = jnp.concatenate(outs, axis=0)
    o = o * lax.rsqrt(jnp.mean(o * o, axis=-1, keepdims=True) + EPS) * gn_ref[...]
    g = gate_ref[...].astype(F32)
    o_ref[...] = (o * (g * jax.nn.sigmoid(g))).astype(o_ref.dtype)


def gla_branch(proj, lr, w_lr, b_gate, gn, *, heads, q_col, k_col, v_col, gate_col, tc=256):
    s = proj.shape[0]
    tc = min(tc, s)
    blk = lambda col: pl.BlockSpec((tc, HEAD_DIM), lambda h, t: (t, col + h))
    return pl.pallas_call(
        _gla_kernel,
        grid=(heads, s // tc),
        in_specs=[blk(q_col), blk(k_col), blk(v_col), blk(gate_col),
                  pl.BlockSpec((tc, LANES), lambda h, t: (t, 0)),
                  pl.BlockSpec((LANES, HEAD_DIM), lambda h, t: (0, h)),
                  pl.BlockSpec((1, HEAD_DIM), lambda h, t: (0, h)),
                  pl.BlockSpec((1, HEAD_DIM), lambda h, t: (0, 0))],
        out_specs=pl.BlockSpec((tc, HEAD_DIM), lambda h, t: (t, h)),
        out_shape=jax.ShapeDtypeStruct((s, heads * HEAD_DIM), BF16),
        scratch_shapes=[pltpu.VMEM((HEAD_DIM, HEAD_DIM), F32)],
        compiler_params=_params("arbitrary", "arbitrary"),
        name="gla_branch",
    )(proj, proj, proj, proj, lr, w_lr, b_gate.reshape(1, -1).astype(F32), gn.reshape(1, -1).astype(F32))


def _merge_kernel(ya_ref, yb_ref, yc_ref, wa_ref, wb_ref, wc_ref, ga_ref, gb_ref, gc_ref, o_ref):
    def term(y_ref, w_ref, g_ref):
        return (jax.nn.sigmoid(g_ref[...].astype(F32))
                * jnp.dot(y_ref[...], w_ref[...], preferred_element_type=F32))
    o_ref[...] = (term(ya_ref, wa_ref, ga_ref) + term(yb_ref, wb_ref, gb_ref)
                  + term(yc_ref, wc_ref, gc_ref)).astype(o_ref.dtype)


def merge_branches(ya, yb, yc, wa, wb, wc, gates, *, bm=512, bn=1024):
    s, w = ya.shape
    d = wa.shape[1]
    bm, bn = min(bm, s), min(bn, d)
    nb = d // bn
    y_spec = pl.BlockSpec((bm, w), lambda i, j: (i, 0))
    w_spec = pl.BlockSpec((w, bn), lambda i, j: (0, j))
    g_spec = lambda b: pl.BlockSpec((bm, bn), lambda i, j: (i, b * nb + j))
    return pl.pallas_call(
        _merge_kernel,
        grid=(s // bm, nb),
        in_specs=[y_spec, y_spec, y_spec, w_spec, w_spec, w_spec, g_spec(0), g_spec(1), g_spec(2)],
        out_specs=pl.BlockSpec((bm, bn), lambda i, j: (i, j)),
        out_shape=jax.ShapeDtypeStruct((s, d), BF16),
        compiler_params=_params("arbitrary", "arbitrary"),
        name="merge_branches",
    )(ya, yb, yc, wa, wb, wc, gates, gates, gates)


def _xattn_kernel(q_ref, k_ref, v_ref, o_ref, *, heads):
    hd = q_ref.shape[1] // heads
    scale = hd ** -0.5
    outs = []
    for h in range(heads):
        sl = slice(h * hd, (h + 1) * hd)
        s = lax.dot_general(q_ref[:, sl], k_ref[:, sl], NT_DIMS, preferred_element_type=F32) * scale
        s = s - jnp.max(s, axis=-1, keepdims=True)
        e = jnp.exp(s)
        p = e / jnp.sum(e, axis=-1, keepdims=True)
        outs.append(jnp.dot(p.astype(BF16), v_ref[:, sl], preferred_element_type=F32))
    o_ref[...] = jnp.concatenate(outs, axis=1).astype(o_ref.dtype)


def cross_attention(q, k, v, *, heads, tq=512):
    s, w = q.shape
    m = k.shape[0]
    tq = min(tq, s)
    return pl.pallas_call(
        functools.partial(_xattn_kernel, heads=heads),
        grid=(s // tq,),
        in_specs=[pl.BlockSpec((tq, w), lambda i: (i, 0)),
                  pl.BlockSpec((m, w), lambda i: (0, 0)),
                  pl.BlockSpec((m, w), lambda i: (0, 0))],
        out_specs=pl.BlockSpec((tq, w), lambda i: (i, 0)),
        out_shape=jax.ShapeDtypeStruct((s, w), BF16),
        compiler_params=_params("arbitrary"),
        name="cross_attention",
    )(q, k, v)


def _topk_rows(s, k):
    n = s.shape[0]
    rows = lax.broadcasted_iota(jnp.int32, s.shape, 0).astype(F32)
    rank = jnp.full(s.shape, NOT_RANKED, F32)
    vals, idxs = [], []
    for it in range(k):
        m = jnp.max(s, axis=0, keepdims=True)
        idx = jnp.min(jnp.where(s == m, rows, float(n)), axis=0, keepdims=True)
        hit = rows == idx
        rank = jnp.where(hit, float(it), rank)
        s = jnp.where(hit, -jnp.inf, s)
        vals.append(m)
        idxs.append(idx)
    return vals, idxs, rank


def _peer_select_kernel(q_ref, keys_ref, e0_ref, c0_ref, e1_ref, r1_ref):
    qd = keys_ref.shape[3]
    part = []
    for p in range(2):
        qh, ql = _split_bf16(q_ref[:, p * qd:(p + 1) * qd])
        kh, kl = _split_bf16(keys_ref[0, p])
        s = (lax.dot_general(kh, qh, NT_DIMS, preferred_element_type=F32)
             + lax.dot_general(kh, ql, NT_DIMS, preferred_element_type=F32)
             + lax.dot_general(kl, qh, NT_DIMS, preferred_element_type=F32))
        vals, _, rank = _topk_rows(s, PEER_TOPK)
        part.append((s, vals, rank))
    (s0, v0, rank0), (s1, v1, rank1) = part
    pairs = [(a, b) for a in range(PEER_TOPK) for b in range(PEER_TOPK) if (a + 1) * (b + 1) <= PEER_TOPK]
    cand = jnp.concatenate([v0[a] + v1[b] for a, b in pairs], axis=0)
    pad = (-len(pairs)) % SUBLANES
    if pad:
        cand = jnp.concatenate([cand, jnp.full((pad, cand.shape[1]), -jnp.inf, F32)], axis=0)
    best, _, rank_c = _topk_rows(cand, PEER_TOPK)
    z = jnp.zeros_like(best[0])
    for b in best:
        z = z + jnp.exp(b - best[0])
    chosen = jnp.where(rank_c < NOT_RANKED, 1.0, 0.0)
    c0 = jnp.zeros_like(s0)
    crow = lax.broadcasted_iota(jnp.int32, chosen.shape, 0)
    for a in range(PEER_TOPK):
        rows_a = [r for r, (aa, _) in enumerate(pairs) if aa == a]
        in_a = (crow >= rows_a[0]) & (crow <= rows_a[-1])
        cnt = jnp.sum(jnp.where(in_a, chosen, 0.0), axis=0, keepdims=True)
        c0 = jnp.where(rank0 == float(a), cnt, c0)
    e0_ref[...] = jnp.exp(s0 - v0[0])
    c0_ref[...] = c0
    e1_ref[...] = (jnp.exp(s1 - v1[0]) / z).astype(e1_ref.dtype)
    r1_ref[...] = rank1.astype(r1_ref.dtype)


def peer_select(q, keys, *, tb=512):
    t = q.shape[0]
    h, _, nk, qd = keys.shape
    tb = min(tb, t)
    tab = jax.ShapeDtypeStruct((h * nk, t), F32)
    tab_bf16 = jax.ShapeDtypeStruct((h * nk, t), BF16)
    tab_spec = pl.BlockSpec((nk, tb), lambda i, hh: (hh, i))
    return pl.pallas_call(
        _peer_select_kernel,
        grid=(t // tb, h),
        in_specs=[pl.BlockSpec((tb, 2 * qd), lambda i, hh: (i, hh)),
                  pl.BlockSpec((1, 2, nk, qd), lambda i, hh: (hh, 0, 0, 0))],
        out_specs=[tab_spec] * 4,
        out_shape=[tab, tab, tab_bf16, tab_bf16],
        compiler_params=_params("arbitrary", "arbitrary"),
        name="peer_select",
    )(q, keys)


def _peer_expert_kernel(h_ref, down_ref, up_ref, e0_ref, c0_ref, e1_ref, r1_ref, o_ref, *, heads, nk):
    e = pl.program_id(1)
    eb = down_ref.shape[0]

    @pl.when(e == 0)
    def _():
        o_ref[...] = jnp.zeros_like(o_ref)

    act = lax.dot_general(down_ref[...], h_ref[...], NT_DIMS, preferred_element_type=F32)
    act = _gelu_tanh(act).astype(BF16)
    w_rows = []
    for il in range(eb // nk):
        i = e * (eb // nk) + il
        w = None
        for h in range(heads):
            row = pl.ds(h * nk + i, 1)
            blk = slice(h * nk, (h + 1) * nk)
            count = c0_ref[row, :].astype(BF16)
            gate0 = e0_ref[row, :].astype(BF16)
            contrib = jnp.where(r1_ref[blk, :] < count, gate0 * e1_ref[blk, :], jnp.zeros((), BF16))
            w = contrib if w is None else w + contrib
        w_rows.append(w)
    xt = jnp.concatenate(w_rows, axis=0) * act
    o_ref[...] += lax.dot_general(xt, up_ref[...], TN_DIMS, preferred_element_type=F32)


def peer_experts(h, down, up, tables, *, heads, nk, tb=512, eb=512):
    t, d = h.shape
    n_exp = down.shape[0]
    tb, eb = min(tb, t), min(eb, n_exp)
    once = dict(pipeline_mode=pl.Buffered(1))
    tab_spec = pl.BlockSpec((heads * nk, tb), lambda i, e: (0, i), **once)
    return pl.pallas_call(
        functools.partial(_peer_expert_kernel, heads=heads, nk=nk),
        grid=(t // tb, n_exp // eb),
        in_specs=[pl.BlockSpec((tb, d), lambda i, e: (i, 0), **once),
                  pl.BlockSpec((eb, d), lambda i, e: (e, 0)),
                  pl.BlockSpec((eb, d), lambda i, e: (e, 0))] + [tab_spec] * 4,
        out_specs=pl.BlockSpec((tb, d), lambda i, e: (i, 0)),
        out_shape=jax.ShapeDtypeStruct((t, d), F32),
        compiler_params=_params("arbitrary", "arbitrary"),
        name="peer_experts",
    )(h, down, up, *tables)


def _cast_kernel(w_ref, o_ref):
    o_ref[...] = w_ref[0].astype(o_ref.dtype)


def _cast_shift_kernel(a_ref, b_ref, o_ref, *, shift):
    win = jnp.concatenate([a_ref[0], b_ref[0]], axis=1)
    o_ref[...] = win[:, shift:shift + o_ref.shape[1]].astype(o_ref.dtype)


def stage_w_in(w_in, layer, n_main, rank, *, br=512, bc=1024):
    _, d, total = w_in.shape
    n_gates = total - n_main - rank
    assert n_main % bc == 0 and n_gates % bc == 0 and rank < LANES
    main = pl.pallas_call(
        _cast_kernel,
        grid=(d // br, n_main // bc),
        in_specs=[pl.BlockSpec((1, br, bc), lambda i, j: (layer, i, j))],
        out_specs=pl.BlockSpec((br, bc), lambda i, j: (i, j)),
        out_shape=jax.ShapeDtypeStruct((d, n_main), BF16),
        compiler_params=_params("arbitrary", "arbitrary"),
        name="stage_w_main",
    )(w_in)
    lr_blk = pl.pallas_call(
        _cast_kernel,
        grid=(d // br,),
        in_specs=[pl.BlockSpec((1, br, LANES), lambda i: (layer, i, n_main // LANES))],
        out_specs=pl.BlockSpec((br, LANES), lambda i: (i, 0)),
        out_shape=jax.ShapeDtypeStruct((d, LANES), BF16),
        compiler_params=_params("arbitrary"),
        name="stage_w_lr",
    )(w_in)
    gates = pl.pallas_call(
        functools.partial(_cast_shift_kernel, shift=rank),
        grid=(d // br, n_gates // bc),
        in_specs=[pl.BlockSpec((1, br, bc), lambda i, j: (layer, i, n_main // bc + j)),
                  pl.BlockSpec((1, br, LANES), lambda i, j: (layer, i, (n_main + (j + 1) * bc) // LANES))],
        out_specs=pl.BlockSpec((br, bc), lambda i, j: (i, j)),
        out_shape=jax.ShapeDtypeStruct((d, n_gates), BF16),
        compiler_params=_params("arbitrary", "arbitrary"),
        name="stage_w_gates",
    )(w_in, w_in)
    return main, lr_blk, gates


def kernel(x, mem, norm_mix, w_in, conv_w, conv_b, lru_wr, lru_br, lru_wi, lru_bi, lru_lambda, gla_w_lr, gla_b_gate, gla_norm, w_pa, w_pb, w_pc, w_out, norm_cross, norm_mem, x_wq, x_wk, x_wv, x_wo, norm_ffn, peer_wq, peer_keys, peer_down, peer_up, norm_final):
    depth = w_in.shape[0]
    batch, seq, d = x.shape
    assert batch == 1
    width = conv_w.shape[2]
    heads = width // HEAD_DIM
    rank = gla_w_lr.shape[1]
    n_main = 9 * width
    col = lambda k: k * heads

    xs = x.reshape(seq, d)
    mem2 = mem.reshape(mem.shape[1], d)
    deltas = []
    for l in range(depth):
        w_main, w_lr, w_gates = stage_w_in(w_in, l, n_main, rank)

        if l == 0:
            h = resnorm(xs, [], norm_mix[l], emit_x=False, out_dtype=BF16)
        else:
            xs, h = resnorm(xs, deltas, norm_mix[l], emit_x=True, out_dtype=BF16)
        proj = matmul(h, w_main, BF16)
        lr = matmul(h, w_lr, BF16, bn=LANES)
        gates = matmul(h, w_gates, BF16)

        ya = lru_branch(proj, conv_w[l], conv_b[l], lru_wr[l], lru_br[l], lru_wi[l], lru_bi[l],
                        lru_lambda[l], width=width)
        yb = stick_breaking(proj, heads=heads, q_col=col(2), k_col=col(3), v_col=col(4))
        w_lr_gate = jnp.pad(gla_w_lr[l], ((0, LANES - rank), (0, 0))).astype(BF16)
        yc = gla_branch(proj, lr, w_lr_gate, gla_b_gate[l], gla_norm[l], heads=heads,
                        q_col=col(5), k_col=col(6), v_col=col(7), gate_col=col(8))
        mixed = merge_branches(ya, yb, yc, w_pa[l].astype(BF16), w_pb[l].astype(BF16),
                               w_pc[l].astype(BF16), gates)
        delta = matmul(mixed, w_out[l].astype(BF16), F32)

        xs, h = resnorm(xs, [delta], norm_cross[l], emit_x=True, out_dtype=BF16)
        m = resnorm(mem2, [], norm_mem[l], emit_x=False, out_dtype=BF16)
        q = matmul(h, x_wq[l].astype(BF16), BF16)
        k = matmul(m, x_wk[l].astype(BF16), BF16)
        v = matmul(m, x_wv[l].astype(BF16), BF16)
        o = cross_attention(q, k, v, heads=X_HEADS)
        delta = matmul(o, x_wo[l].astype(BF16), F32)

        xs, h = resnorm(xs, [delta], norm_ffn[l], emit_x=True, out_dtype=BF16)
        pq = matmul(h, peer_wq[l].astype(BF16), F32)
        tables = peer_select(pq, peer_keys[l])
        delta = peer_experts(h, peer_down[l].astype(BF16), peer_up[l].astype(BF16), tables,
                             heads=peer_keys.shape[1], nk=peer_keys.shape[3])
        deltas = [delta]

    out = resnorm(xs, deltas, norm_final, emit_x=False, out_dtype=F32)
    return out.reshape(batch, seq, d)
```

```python
import functools
import math

import jax
import jax.numpy as jnp
from jax import lax
from jax.experimental import pallas as pl
from jax.experimental.pallas import tpu as pltpu

F32 = jnp.float32
BF16 = jnp.bfloat16

LANES = 128
SUBLANES = 8
VMEM_LIMIT_BYTES = 56 * 1024 * 1024

EPS = 1e-6
HEAD_DIM = 128
LRU_C = 8.0
GLA_CHUNK = 64
GLA_GATE_TEMP = 16.0
X_HEADS = 4
PEER_TOPK = 16
EXP_UNDERFLOW = -105.0
NOT_RANKED = 4096.0

NT_DIMS = (((1,), (1,)), ((), ()))
TN_DIMS = (((0,), (0,)), ((), ()))


def _params(*semantics, flags=None):
    return pltpu.CompilerParams(dimension_semantics=semantics, vmem_limit_bytes=VMEM_LIMIT_BYTES,
                                flags=flags)


def _gelu_tanh(x):
    return 0.5 * x * (1.0 + jnp.tanh(math.sqrt(2.0 / math.pi) * (x + 0.044715 * (x * x * x))))


def _softplus(x):
    return jnp.maximum(x, 0.0) + jnp.log1p(jnp.exp(-jnp.abs(x)))


def _split_bf16(x):
    hi = x.astype(BF16)
    lo = (x - hi.astype(F32)).astype(BF16)
    return hi, lo


def _resnorm_kernel(*refs, n_delta, emit_x):
    x_ref = refs[0]
    d_refs = refs[1:1 + n_delta]
    g_ref = refs[1 + n_delta]
    outs = refs[2 + n_delta:]
    x = x_ref[...]
    for d in d_refs:
        x = x + d[...].astype(F32)
    y = x * lax.rsqrt(jnp.mean(x * x, axis=-1, keepdims=True) + EPS) * g_ref[...]
    if emit_x:
        outs[0][...] = x
        outs[1][...] = y.astype(outs[1].dtype)
    else:
        outs[0][...] = y.astype(outs[0].dtype)


def resnorm(x, deltas, g, *, emit_x, out_dtype, block_rows=256):
    rows, d = x.shape
    block_rows = min(block_rows, rows)
    row_spec = pl.BlockSpec((block_rows, d), lambda i: (i, 0))
    out_shape = [jax.ShapeDtypeStruct((rows, d), out_dtype)]
    out_specs = [row_spec]
    if emit_x:
        out_shape = [jax.ShapeDtypeStruct((rows, d), F32)] + out_shape
        out_specs = [row_spec] + out_specs
    res = pl.pallas_call(
        functools.partial(_resnorm_kernel, n_delta=len(deltas), emit_x=emit_x),
        grid=(rows // block_rows,),
        in_specs=[row_spec] * (1 + len(deltas)) + [pl.BlockSpec((1, d), lambda i: (0, 0))],
        out_specs=out_specs,
        out_shape=out_shape,
        compiler_params=_params("arbitrary"),
        name="resnorm",
    )(x, *deltas, g.reshape(1, d).astype(F32))
    return res if emit_x else res[0]


def _mm_kernel(a_ref, w_ref, o_ref):
    o_ref[...] = jnp.dot(a_ref[...], w_ref[...], preferred_element_type=F32).astype(o_ref.dtype)


def _mm_nt_kernel(a_ref, wt_ref, o_ref):
    o_ref[...] = lax.dot_general(a_ref[...], wt_ref[...], NT_DIMS,
                                 preferred_element_type=F32).astype(o_ref.dtype)


def matmul(a, w, out_dtype, *, bm=1024, bn=1024, transposed=False, layer=None, n=None):
    m, k = a.shape
    if layer is not None:
        assert transposed and w.ndim == 3 and n % min(bn, n) == 0
    else:
        n = w.shape[0] if transposed else w.shape[1]
    bm, bn = min(bm, m), min(bn, n)
    if layer is not None:
        w_spec = pl.BlockSpec((None, bn, k), lambda i, j: (layer, j, 0))
    elif transposed:
        w_spec = pl.BlockSpec((bn, k), lambda i, j: (j, 0))
    else:
        w_spec = pl.BlockSpec((k, bn), lambda i, j: (0, j))
    return pl.pallas_call(
        _mm_nt_kernel if transposed else _mm_kernel,
        grid=(m // bm, n // bn),
        in_specs=[pl.BlockSpec((bm, k), lambda i, j: (i, 0)), w_spec],
        out_specs=pl.BlockSpec((bm, bn), lambda i, j: (i, j)),
        out_shape=jax.ShapeDtypeStruct((m, n), out_dtype),
        compiler_params=_params("arbitrary", "arbitrary"),
        name="matmul",
    )(a, w)


def _lru_kernel(ax_ref, gate_ref, cw_ref, cb_ref, wr_ref, br_ref, wi_ref, bi_ref, lam_ref,
                o_ref, tail_ref, hlast_ref, a_scr, u_scr, *, heads_per_block, conv_width):
    t = pl.program_id(1)
    tb, cb = ax_ref.shape

    @pl.when(t == 0)
    def _():
        tail_ref[...] = jnp.zeros_like(tail_ref)
        hlast_ref[...] = jnp.zeros_like(hlast_ref)

    x = ax_ref[...].astype(F32)
    prev = tail_ref[...]
    row8 = lax.broadcasted_iota(jnp.int32, (SUBLANES, cb), 0)
    cw = cw_ref[...]
    y = x * cw[conv_width - 1:conv_width, :] + cb_ref[...]
    for d in range(1, conv_width):
        rolled = pltpu.roll(x, d, 0)
        first = jnp.where(row8 < d, pltpu.roll(prev, d, 0), rolled[:SUBLANES])
        shifted = jnp.concatenate([first, rolled[SUBLANES:]], axis=0)
        y = y + shifted * cw[conv_width - 1 - d:conv_width - d, :]
    tail_ref[...] = x[tb - SUBLANES:, :]

    yb = y.astype(BF16)
    r_parts, i_parts = [], []
    for hh in range(heads_per_block):
        yh = yb[:, hh * HEAD_DIM:(hh + 1) * HEAD_DIM]
        r_parts.append(jnp.dot(yh, wr_ref[hh], preferred_element_type=F32))
        i_parts.append(jnp.dot(yh, wi_ref[hh], preferred_element_type=F32))
    r = jax.nn.sigmoid(jnp.concatenate(r_parts, axis=1) + br_ref[...])
    i = jax.nn.sigmoid(jnp.concatenate(i_parts, axis=1) + bi_ref[...])
    log_a = (-LRU_C) * r * _softplus(-lam_ref[...])
    a = jnp.exp(log_a)
    a_scr[...] = a
    u_scr[...] = jnp.sqrt(-jnp.tanh(log_a) * (a * a + 1.0)) * (i * y)

    def group(gidx, h_prev):
        rows = pl.ds(pl.multiple_of(gidx * SUBLANES, SUBLANES), SUBLANES)
        a = a_scr[rows, :]
        u = u_scr[rows, :]
        for d in (1, 2, 4):
            keep = row8 >= d
            u = jnp.where(keep, a * pltpu.roll(u, d, 0) + u, u)
            a = jnp.where(keep, a * pltpu.roll(a, d, 0), a)
        h = a * h_prev + u
        u_scr[rows, :] = h
        return h[SUBLANES - 1:SUBLANES, :]

    h_last = lax.fori_loop(0, tb // SUBLANES, group, hlast_ref[0:1, :])
    hlast_ref[0:1, :] = h_last
    o_ref[...] = (u_scr[...] * _gelu_tanh(gate_ref[...].astype(F32))).astype(o_ref.dtype)


def lru_branch(proj, conv_w, conv_b, wr, br, wi, bi, lam, *, width, tb=512, cb=256):
    s = proj.shape[0]
    tb = min(tb, s)
    nc = width // cb
    hpb = cb // HEAD_DIM
    kw = conv_w.shape[0]
    vec = lambda a: a.reshape(1, width).astype(F32)
    vec_spec = pl.BlockSpec((1, cb), lambda c, t: (0, c))
    return pl.pallas_call(
        functools.partial(_lru_kernel, heads_per_block=hpb, conv_width=kw),
        grid=(nc, s // tb),
        in_specs=[pl.BlockSpec((tb, cb), lambda c, t: (t, c)),
                  pl.BlockSpec((tb, cb), lambda c, t: (t, nc + c)),
                  pl.BlockSpec((kw, cb), lambda c, t: (0, c)),
                  vec_spec,
                  pl.BlockSpec((hpb, HEAD_DIM, HEAD_DIM), lambda c, t: (c, 0, 0)),
                  vec_spec,
                  pl.BlockSpec((hpb, HEAD_DIM, HEAD_DIM), lambda c, t: (c, 0, 0)),
                  vec_spec, vec_spec],
        out_specs=pl.BlockSpec((tb, cb), lambda c, t: (t, c)),
        out_shape=jax.ShapeDtypeStruct((s, width), BF16),
        scratch_shapes=[pltpu.VMEM((SUBLANES, cb), F32), pltpu.VMEM((SUBLANES, cb), F32),
                        pltpu.VMEM((tb, cb), F32), pltpu.VMEM((tb, cb), F32)],
        compiler_params=_params("arbitrary", "arbitrary"),
        name="lru_branch",
    )(proj, proj, conv_w.astype(F32), vec(conv_b), wr.astype(BF16), vec(br), wi.astype(BF16),
      vec(bi), vec(lam))


def _stick_kernel(q_ref, k_ref, v_ref, o_ref, c_scr, acc_scr, *, tk, group):
    qi = pl.program_id(1)
    tq = q_ref.shape[0]
    n_diag = tq // tk
    scale = HEAD_DIM ** -0.5
    rr = lax.broadcasted_iota(jnp.int32, (tk, 2 * tk), 0)
    cc = lax.broadcasted_iota(jnp.int32, (tk, 2 * tk), 1)
    cum_mat = jnp.where((rr > cc) | (cc >= tk), 1.0, 0.0).astype(BF16)

    def block(j, g, c, causal_mask):
        rows = pl.ds(pl.multiple_of(j * tk, tk), tk)
        lanes = slice(g * HEAD_DIM, (g + 1) * HEAD_DIM)
        z = lax.dot_general(q_ref[:, lanes], k_ref[rows, lanes], NT_DIMS,
                            preferred_element_type=F32) * scale
        sp = _softplus(z)
        log_keep = -sp
        log_beta = z - sp
        if causal_mask is not None:
            log_keep = jnp.where(causal_mask, log_keep, 0.0)
        hi, lo = _split_bf16(log_keep)
        cum = (jnp.dot(hi, cum_mat, preferred_element_type=F32)
               + jnp.dot(lo, cum_mat, preferred_element_type=F32))
        w = jnp.exp(log_beta + cum[:, :tk] + c)
        if causal_mask is not None:
            w = jnp.where(causal_mask, w, 0.0)
        pv = jnp.dot(w.astype(BF16), v_ref[rows, lanes], preferred_element_type=F32)
        return pv, c + cum[:, tk:]

    qrow = lax.broadcasted_iota(jnp.int32, (tq, tk), 0)
    kcol = lax.broadcasted_iota(jnp.int32, (tq, tk), 1)
    cs = [jnp.zeros((tq, tk), F32) for _ in range(group)]
    accs = [jnp.zeros((tq, HEAD_DIM), F32) for _ in range(group)]
    for dj in reversed(range(n_diag)):
        mask = (kcol + dj * tk) < qrow
        for g in range(group):
            pv, cs[g] = block(qi * n_diag + dj, g, cs[g], mask)
            accs[g] = accs[g] + pv
    cmax = jnp.max(cs[0])
    for g in range(group):
        c_scr[g] = cs[g]
        acc_scr[g] = accs[g]
        if g:
            cmax = jnp.maximum(cmax, jnp.max(cs[g]))

    def cond(carry):
        j, cmax = carry
        return jnp.logical_and(j >= 0, cmax > EXP_UNDERFLOW)

    def body(carry):
        j, _ = carry
        cmax = None
        for g in range(group):
            pv, c = block(j, g, c_scr[g], None)
            acc_scr[g] += pv
            c_scr[g] = c
            m = jnp.max(c)
            cmax = m if cmax is None else jnp.maximum(cmax, m)
        return j - 1, cmax

    lax.while_loop(cond, body, (qi * n_diag - 1, cmax))
    for g in range(group):
        o_ref[:, g * HEAD_DIM:(g + 1) * HEAD_DIM] = acc_scr[g].astype(o_ref.dtype)


def stick_breaking(proj, *, heads, q_col, k_col, v_col, tq=256, tk=256, group=4):
    s = proj.shape[0]
    tq = min(tq, s)
    gw = group * HEAD_DIM
    assert heads % group == 0 and q_col % group == 0 and k_col % group == 0 and v_col % group == 0
    return pl.pallas_call(
        functools.partial(_stick_kernel, tk=tk, group=group),
        grid=(heads // group, s // tq),
        in_specs=[pl.BlockSpec((tq, gw), lambda h, i: (i, q_col // group + h)),
                  pl.BlockSpec((s, gw), lambda h, i: (0, k_col // group + h)),
                  pl.BlockSpec((s, gw), lambda h, i: (0, v_col // group + h))],
        out_specs=pl.BlockSpec((tq, gw), lambda h, i: (i, h)),
        out_shape=jax.ShapeDtypeStruct((s, heads * HEAD_DIM), BF16),
        scratch_shapes=[pltpu.VMEM((group, tq, tk), F32), pltpu.VMEM((group, tq, HEAD_DIM), F32)],
        compiler_params=_params("arbitrary", "arbitrary"),
        name="stick_breaking",
    )(proj, proj, proj)


def _gla_kernel(q_ref, k_ref, v_ref, gate_ref, lr_ref, wlr_ref, bg_ref, gn_ref, o_ref, state_scr):
    t = pl.program_id(1)
    tc = q_ref.shape[0]

    @pl.when(t == 0)
    def _():
        state_scr[...] = jnp.zeros_like(state_scr)

    pre = jnp.dot(lr_ref[...], wlr_ref[...], preferred_element_type=F32) + bg_ref[...]
    log_g = -_softplus(-pre) * (1.0 / GLA_GATE_TEMP)

    rr = lax.broadcasted_iota(jnp.int32, (tc, tc), 0)
    cc = lax.broadcasted_iota(jnp.int32, (tc, tc), 1)
    same = (rr // GLA_CHUNK) == (cc // GLA_CHUNK)
    cum_mat = jnp.where(same & (cc <= rr), 1.0, 0.0).astype(BF16)
    tot_mat = jnp.where(same, 1.0, 0.0).astype(BF16)
    hi, lo = _split_bf16(log_g)
    b = jnp.dot(cum_mat, hi, preferred_element_type=F32) + jnp.dot(cum_mat, lo, preferred_element_type=F32)
    b_last = jnp.dot(tot_mat, hi, preferred_element_type=F32) + jnp.dot(tot_mat, lo, preferred_element_type=F32)

    q = q_ref[...].astype(F32) * (HEAD_DIM ** -0.5)
    k = k_ref[...].astype(F32)
    q_dec = (q * jnp.exp(b)).astype(BF16)
    k_in = (k * jnp.exp(-b)).astype(BF16)
    k_state = (k * jnp.exp(b_last - b)).astype(BF16)
    v = v_ref[...]
    tril = (lax.broadcasted_iota(jnp.int32, (GLA_CHUNK, GLA_CHUNK), 1)
            <= lax.broadcasted_iota(jnp.int32, (GLA_CHUNK, GLA_CHUNK), 0))

    state_t = state_scr[...]
    outs = []
    for ci in range(tc // GLA_CHUNK):
        sl = slice(ci * GLA_CHUNK, (ci + 1) * GLA_CHUNK)
        p = lax.dot_general(q_dec[sl], k_in[sl], NT_DIMS, preferred_element_type=F32)
        p = jnp.where(tril, p, 0.0)
        o = jnp.dot(p.astype(BF16), v[sl], preferred_element_type=F32)
        o = o + lax.dot_general(q_dec[sl], state_t.astype(BF16), NT_DIMS, preferred_element_type=F32)
        decay = jnp.exp(b_last[ci * GLA_CHUNK:ci * GLA_CHUNK + 1, :])
        state_t = state_t * decay + lax.dot_general(v[sl], k_state[sl], TN_DIMS,
                                                    preferred_element_type=F32)
        outs.append(o)
    state_scr[...] = state_t
    o = jnp.concatenate(outs, axis=0)
    o = o * lax.rsqrt(jnp.mean(o * o, axis=-1, keepdims=True) + EPS) * gn_ref[...]
    g = gate_ref[...].astype(F32)
    o_ref[...] = (o * (g * jax.nn.sigmoid(g))).astype(o_ref.dtype)


def gla_branch(proj, lr, w_lr, b_gate, gn, *, heads, q_col, k_col, v_col, gate_col, tc=256):
    s = proj.shape[0]
    tc = min(tc, s)
    blk = lambda col: pl.BlockSpec((tc, HEAD_DIM), lambda h, t: (t, col + h))
    return pl.pallas_call(
        _gla_kernel,
        grid=(heads, s // tc),
        in_specs=[blk(q_col), blk(k_col), blk(v_col), blk(gate_col),
                  pl.BlockSpec((tc, LANES), lambda h, t: (t, 0)),
                  pl.BlockSpec((LANES, HEAD_DIM), lambda h, t: (0, h)),
                  pl.BlockSpec((1, HEAD_DIM), lambda h, t: (0, h)),
                  pl.BlockSpec((1, HEAD_DIM), lambda h, t: (0, 0))],
        out_specs=pl.BlockSpec((tc, HEAD_DIM), lambda h, t: (t, h)),
        out_shape=jax.ShapeDtypeStruct((s, heads * HEAD_DIM), BF16),
        scratch_shapes=[pltpu.VMEM((HEAD_DIM, HEAD_DIM), F32)],
        compiler_params=_params("arbitrary", "arbitrary"),
        name="gla_branch",
    )(proj, proj, proj, proj, lr, w_lr, b_gate.reshape(1, -1).astype(F32), gn.reshape(1, -1).astype(F32))


def _merge_kernel(ya_ref, yb_ref, yc_ref, wa_ref, wb_ref, wc_ref, ga_ref, gb_ref, gc_ref, o_ref):
    def term(y_ref, w_ref, g_ref):
        return (jax.nn.sigmoid(g_ref[...].astype(F32))
                * jnp.dot(y_ref[...], w_ref[...], preferred_element_type=F32))
    o_ref[...] = (term(ya_ref, wa_ref, ga_ref) + term(yb_ref, wb_ref, gb_ref)
                  + term(yc_ref, wc_ref, gc_ref)).astype(o_ref.dtype)


def merge_branches(ya, yb, yc, wa, wb, wc, gates, *, bm=512, bn=1024):
    s, w = ya.shape
    d = wa.shape[1]
    bm, bn = min(bm, s), min(bn, d)
    nb = d // bn
    y_spec = pl.BlockSpec((bm, w), lambda i, j: (i, 0))
    w_spec = pl.BlockSpec((w, bn), lambda i, j: (0, j))
    g_spec = lambda b: pl.BlockSpec((bm, bn), lambda i, j: (i, b * nb + j))
    return pl.pallas_call(
        _merge_kernel,
        grid=(s // bm, nb),
        in_specs=[y_spec, y_spec, y_spec, w_spec, w_spec, w_spec, g_spec(0), g_spec(1), g_spec(2)],
        out_specs=pl.BlockSpec((bm, bn), lambda i, j: (i, j)),
        out_shape=jax.ShapeDtypeStruct((s, d), BF16),
        compiler_params=_params("arbitrary", "arbitrary"),
        name="merge_branches",
    )(ya, yb, yc, wa, wb, wc, gates, gates, gates)


def _xattn_kernel(q_ref, k_ref, v_ref, o_ref, *, heads):
    hd = q_ref.shape[1] // heads
    scale = hd ** -0.5
    outs = []
    for h in range(heads):
        sl = slice(h * hd, (h + 1) * hd)
        s = lax.dot_general(q_ref[:, sl], k_ref[:, sl], NT_DIMS, preferred_element_type=F32) * scale
        s = s - jnp.max(s, axis=-1, keepdims=True)
        e = jnp.exp(s)
        p = e / jnp.sum(e, axis=-1, keepdims=True)
        outs.append(jnp.dot(p.astype(BF16), v_ref[:, sl], preferred_element_type=F32))
    o_ref[...] = jnp.concatenate(outs, axis=1).astype(o_ref.dtype)


def cross_attention(q, k, v, *, heads, tq=512):
    s, w = q.shape
    m = k.shape[0]
    tq = min(tq, s)
    return pl.pallas_call(
        functools.partial(_xattn_kernel, heads=heads),
        grid=(s // tq,),
        in_specs=[pl.BlockSpec((tq, w), lambda i: (i, 0)),
                  pl.BlockSpec((m, w), lambda i: (0, 0)),
                  pl.BlockSpec((m, w), lambda i: (0, 0))],
        out_specs=pl.BlockSpec((tq, w), lambda i: (i, 0)),
        out_shape=jax.ShapeDtypeStruct((s, w), BF16),
        compiler_params=_params("arbitrary"),
        name="cross_attention",
    )(q, k, v)


def _topk_rows(s, k):
    n = s.shape[0]
    rows = lax.broadcasted_iota(jnp.int32, s.shape, 0).astype(F32)
    rank = jnp.full(s.shape, NOT_RANKED, F32)
    vals, idxs = [], []
    for it in range(k):
        m = jnp.max(s, axis=0, keepdims=True)
        idx = jnp.min(jnp.where(s == m, rows, float(n)), axis=0, keepdims=True)
        hit = rows == idx
        rank = jnp.where(hit, float(it), rank)
        s = jnp.where(hit, -jnp.inf, s)
        vals.append(m)
        idxs.append(idx)
    return vals, idxs, rank


def _peer_select_kernel(q_ref, keys_ref, e0_ref, c0_ref, e1_ref, r1_ref):
    qd = keys_ref.shape[3]
    part = []
    for p in range(2):
        qh, ql = _split_bf16(q_ref[:, p * qd:(p + 1) * qd])
        kh, kl = _split_bf16(keys_ref[0, p])
        s = (lax.dot_general(kh, qh, NT_DIMS, preferred_element_type=F32)
             + lax.dot_general(kh, ql, NT_DIMS, preferred_element_type=F32)
             + lax.dot_general(kl, qh, NT_DIMS, preferred_element_type=F32))
        vals, _, rank = _topk_rows(s, PEER_TOPK)
        part.append((s, vals, rank))
    (s0, v0, rank0), (s1, v1, rank1) = part
    pairs = [(a, b) for a in range(PEER_TOPK) for b in range(PEER_TOPK) if (a + 1) * (b + 1) <= PEER_TOPK]
    cand = jnp.concatenate([v0[a] + v1[b] for a, b in pairs], axis=0)
    pad = (-len(pairs)) % SUBLANES
    if pad:
        cand = jnp.concatenate([cand, jnp.full((pad, cand.shape[1]), -jnp.inf, F32)], axis=0)
    best, _, rank_c = _topk_rows(cand, PEER_TOPK)
    z = jnp.zeros_like(best[0])
    for b in best:
        z = z + jnp.exp(b - best[0])
    chosen = jnp.where(rank_c < NOT_RANKED, 1.0, 0.0)
    c0 = jnp.zeros_like(s0)
    crow = lax.broadcasted_iota(jnp.int32, chosen.shape, 0)
    for a in range(PEER_TOPK):
        rows_a = [r for r, (aa, _) in enumerate(pairs) if aa == a]
        in_a = (crow >= rows_a[0]) & (crow <= rows_a[-1])
        cnt = jnp.sum(jnp.where(in_a, chosen, 0.0), axis=0, keepdims=True)
        c0 = jnp.where(rank0 == float(a), cnt, c0)
    e0_ref[...] = jnp.exp(s0 - v0[0])
    c0_ref[...] = c0
    e1_ref[...] = (jnp.exp(s1 - v1[0]) / z).astype(e1_ref.dtype)
    r1_ref[...] = rank1.astype(r1_ref.dtype)


def peer_select(q, keys, *, tb=512):
    t = q.shape[0]
    h, _, nk, qd = keys.shape
    tb = min(tb, t)
    tab = jax.ShapeDtypeStruct((h * nk, t), F32)
    tab_bf16 = jax.ShapeDtypeStruct((h * nk, t), BF16)
    tab_spec = pl.BlockSpec((nk, tb), lambda i, hh: (hh, i))
    return pl.pallas_call(
        _peer_select_kernel,
        grid=(t // tb, h),
        in_specs=[pl.BlockSpec((tb, 2 * qd), lambda i, hh: (i, hh)),
                  pl.BlockSpec((1, 2, nk, qd), lambda i, hh: (hh, 0, 0, 0))],
        out_specs=[tab_spec] * 4,
        out_shape=[tab, tab, tab_bf16, tab_bf16],
        compiler_params=_params("arbitrary", "arbitrary"),
        name="peer_select",
    )(q, keys)


def _peer_expert_kernel(h_ref, down_ref, up_ref, e0_ref, c0_ref, e1_ref, r1_ref, o_ref, *, heads, nk):
    e = pl.program_id(1)
    eb = down_ref.shape[0]

    @pl.when(e == 0)
    def _():
        o_ref[...] = jnp.zeros_like(o_ref)

    act = lax.dot_general(down_ref[...], h_ref[...], NT_DIMS, preferred_element_type=F32)
    act = _gelu_tanh(act).astype(BF16)
    w_rows = []
    for il in range(eb // nk):
        i = e * (eb // nk) + il
        w = None
        for h in range(heads):
            row = pl.ds(h * nk + i, 1)
            blk = slice(h * nk, (h + 1) * nk)
            count = c0_ref[row, :].astype(BF16)
            gate0 = e0_ref[row, :].astype(BF16)
            contrib = jnp.where(r1_ref[blk, :] < count, gate0 * e1_ref[blk, :], jnp.zeros((), BF16))
            w = contrib if w is None else w + contrib
        w_rows.append(w)
    xt = jnp.concatenate(w_rows, axis=0) * act
    o_ref[...] += lax.dot_general(xt, up_ref[...], TN_DIMS, preferred_element_type=F32)


def peer_experts(h, down, up, tables, *, layer, heads, nk, tb=512, eb=512):
    t, d = h.shape
    n_exp = down.shape[1]
    tb, eb = min(tb, t), min(eb, n_exp)
    once = dict(pipeline_mode=pl.Buffered(1))
    tab_spec = pl.BlockSpec((heads * nk, tb), lambda i, e: (0, i), **once)
    return pl.pallas_call(
        functools.partial(_peer_expert_kernel, heads=heads, nk=nk),
        grid=(t // tb, n_exp // eb),
        in_specs=[pl.BlockSpec((tb, d), lambda i, e: (i, 0), **once),
                  pl.BlockSpec((None, eb, d), lambda i, e: (layer, e, 0)),
                  pl.BlockSpec((None, eb, d), lambda i, e: (layer, e, 0))] + [tab_spec] * 4,
        out_specs=pl.BlockSpec((tb, d), lambda i, e: (i, 0)),
        out_shape=jax.ShapeDtypeStruct((t, d), F32),
        compiler_params=_params("arbitrary", "arbitrary"),
        name="peer_experts",
    )(h, down, up, *tables)


def kernel(x, mem, norm_mix, w_in, conv_w, conv_b, lru_wr, lru_br, lru_wi, lru_bi, lru_lambda, gla_w_lr, gla_b_gate, gla_norm, w_pa, w_pb, w_pc, w_out, norm_cross, norm_mem, x_wq, x_wk, x_wv, x_wo, norm_ffn, peer_wq, peer_keys, peer_down, peer_up, norm_final):
    depth = w_in.shape[0]
    batch, seq, d = x.shape
    assert batch == 1
    width = conv_w.shape[2]
    heads = width // HEAD_DIM
    rank = gla_w_lr.shape[1]
    n_main = 9 * width
    col = lambda k: k * heads

    xs = x.reshape(seq, d)
    mem2 = mem.reshape(mem.shape[1], d)
    w_in_t = jnp.swapaxes(w_in, 1, 2).astype(BF16)
    peer_down_bf16 = peer_down.astype(BF16)
    peer_up_bf16 = peer_up.astype(BF16)
    deltas = []
    for l in range(depth):
        w_lr = jnp.pad(w_in_t[l, n_main:n_main + rank], ((0, LANES - rank), (0, 0)))
        w_gates = w_in_t[l, n_main + rank:]

        if l == 0:
            h = resnorm(xs, [], norm_mix[l], emit_x=False, out_dtype=BF16)
        else:
            xs, h = resnorm(xs, deltas, norm_mix[l], emit_x=True, out_dtype=BF16)
        proj = matmul(h, w_in_t, BF16, transposed=True, layer=l, n=n_main)
        lr = matmul(h, w_lr, BF16, bn=LANES, transposed=True)
        gates = matmul(h, w_gates, BF16, transposed=True)

        ya = lru_branch(proj, conv_w[l], conv_b[l], lru_wr[l], lru_br[l], lru_wi[l], lru_bi[l],
                        lru_lambda[l], width=width)
        yb = stick_breaking(proj, heads=heads, q_col=col(2), k_col=col(3), v_col=col(4))
        w_lr_gate = jnp.pad(gla_w_lr[l], ((0, LANES - rank), (0, 0))).astype(BF16)
        yc = gla_branch(proj, lr, w_lr_gate, gla_b_gate[l], gla_norm[l], heads=heads,
                        q_col=col(5), k_col=col(6), v_col=col(7), gate_col=col(8))
        mixed = merge_branches(ya, yb, yc, w_pa[l].astype(BF16), w_pb[l].astype(BF16),
                               w_pc[l].astype(BF16), gates)
        delta = matmul(mixed, w_out[l].astype(BF16), F32)

        xs, h = resnorm(xs, [delta], norm_cross[l], emit_x=True, out_dtype=BF16)
        m = resnorm(mem2, [], norm_mem[l], emit_x=False, out_dtype=BF16)
        q = matmul(h, x_wq[l].astype(BF16), BF16)
        k = matmul(m, x_wk[l].astype(BF16), BF16)
        v = matmul(m, x_wv[l].astype(BF16), BF16)
        o = cross_attention(q, k, v, heads=X_HEADS)
        delta = matmul(o, x_wo[l].astype(BF16), F32)

        xs, h = resnorm(xs, [delta], norm_ffn[l], emit_x=True, out_dtype=BF16)
        pq = matmul(h, peer_wq[l].astype(BF16), F32)
        tables = peer_select(pq, peer_keys[l])
        delta = peer_experts(h, peer_down_bf16, peer_up_bf16, tables, layer=l,
                             heads=peer_keys.shape[1], nk=peer_keys.shape[3])
        deltas = [delta]

    out = resnorm(xs, deltas, norm_final, emit_x=False, out_dtype=F32)
    return out.reshape(batch, seq, d)
```

```python
import functools
import math

import jax
import jax.numpy as jnp
from jax import lax
from jax.experimental import pallas as pl
from jax.experimental.pallas import tpu as pltpu

F32 = jnp.float32
BF16 = jnp.bfloat16

LANES = 128
SUBLANES = 8
VMEM_LIMIT_BYTES = 56 * 1024 * 1024

EPS = 1e-6
HEAD_DIM = 128
LRU_C = 8.0
GLA_CHUNK = 64
GLA_GATE_TEMP = 16.0
X_HEADS = 4
PEER_TOPK = 16
EXP_UNDERFLOW = -105.0
NOT_RANKED = 4096.0

NT_DIMS = (((1,), (1,)), ((), ()))
TN_DIMS = (((0,), (0,)), ((), ()))


def _params(*semantics, flags=None):
    return pltpu.CompilerParams(dimension_semantics=semantics, vmem_limit_bytes=VMEM_LIMIT_BYTES,
                                flags=flags)


def _gelu_tanh(x):
    return 0.5 * x * (1.0 + jnp.tanh(math.sqrt(2.0 / math.pi) * (x + 0.044715 * (x * x * x))))


def _softplus(x):
    return jnp.maximum(x, 0.0) + jnp.log1p(jnp.exp(-jnp.abs(x)))


def _split_bf16(x):
    hi = x.astype(BF16)
    lo = (x - hi.astype(F32)).astype(BF16)
    return hi, lo


def _resnorm_kernel(*refs, n_delta, emit_x):
    x_ref = refs[0]
    d_refs = refs[1:1 + n_delta]
    g_ref = refs[1 + n_delta]
    outs = refs[2 + n_delta:]
    x = x_ref[...]
    for d in d_refs:
        x = x + d[...].astype(F32)
    y = x * lax.rsqrt(jnp.mean(x * x, axis=-1, keepdims=True) + EPS) * g_ref[...]
    if emit_x:
        outs[0][...] = x
        outs[1][...] = y.astype(outs[1].dtype)
    else:
        outs[0][...] = y.astype(outs[0].dtype)


def resnorm(x, deltas, g, *, emit_x, out_dtype, block_rows=256):
    rows, d = x.shape
    block_rows = min(block_rows, rows)
    row_spec = pl.BlockSpec((block_rows, d), lambda i: (i, 0))
    out_shape = [jax.ShapeDtypeStruct((rows, d), out_dtype)]
    out_specs = [row_spec]
    if emit_x:
        out_shape = [jax.ShapeDtypeStruct((rows, d), F32)] + out_shape
        out_specs = [row_spec] + out_specs
    res = pl.pallas_call(
        functools.partial(_resnorm_kernel, n_delta=len(deltas), emit_x=emit_x),
        grid=(rows // block_rows,),
        in_specs=[row_spec] * (1 + len(deltas)) + [pl.BlockSpec((1, d), lambda i: (0, 0))],
        out_specs=out_specs,
        out_shape=out_shape,
        compiler_params=_params("arbitrary"),
        name="resnorm",
    )(x, *deltas, g.reshape(1, d).astype(F32))
    return res if emit_x else res[0]


def _mm_kernel(a_ref, w_ref, o_ref):
    o_ref[...] = jnp.dot(a_ref[...], w_ref[...], preferred_element_type=F32).astype(o_ref.dtype)


def _mm_nt_kernel(a_ref, wt_ref, o_ref):
    o_ref[...] = lax.dot_general(a_ref[...], wt_ref[...], NT_DIMS,
                                 preferred_element_type=F32).astype(o_ref.dtype)


def matmul(a, w, out_dtype, *, bm=1024, bn=1024, transposed=False, layer=None, n=None):
    m, k = a.shape
    if layer is not None:
        assert transposed and w.ndim == 3 and n % min(bn, n) == 0
    else:
        n = w.shape[0] if transposed else w.shape[1]
    bm, bn = min(bm, m), min(bn, n)
    if layer is not None:
        w_spec = pl.BlockSpec((None, bn, k), lambda i, j: (layer, j, 0))
    elif transposed:
        w_spec = pl.BlockSpec((bn, k), lambda i, j: (j, 0))
    else:
        w_spec = pl.BlockSpec((k, bn), lambda i, j: (0, j))
    return pl.pallas_call(
        _mm_nt_kernel if transposed else _mm_kernel,
        grid=(m // bm, n // bn),
        in_specs=[pl.BlockSpec((bm, k), lambda i, j: (i, 0)), w_spec],
        out_specs=pl.BlockSpec((bm, bn), lambda i, j: (i, j)),
        out_shape=jax.ShapeDtypeStruct((m, n), out_dtype),
        compiler_params=_params("arbitrary", "arbitrary"),
        name="matmul",
    )(a, w)


def _lru_kernel(ax_ref, gate_ref, cw_ref, cb_ref, wr_ref, br_ref, wi_ref, bi_ref, lam_ref,
                o_ref, tail_ref, hlast_ref, a_scr, u_scr, *, heads_per_block, conv_width):
    t = pl.program_id(1)
    tb, cb = ax_ref.shape

    @pl.when(t == 0)
    def _():
        tail_ref[...] = jnp.zeros_like(tail_ref)
        hlast_ref[...] = jnp.zeros_like(hlast_ref)

    x = ax_ref[...].astype(F32)
    prev = tail_ref[...]
    row8 = lax.broadcasted_iota(jnp.int32, (SUBLANES, cb), 0)
    cw = cw_ref[...]
    y = x * cw[conv_width - 1:conv_width, :] + cb_ref[...]
    for d in range(1, conv_width):
        rolled = pltpu.roll(x, d, 0)
        first = jnp.where(row8 < d, pltpu.roll(prev, d, 0), rolled[:SUBLANES])
        shifted = jnp.concatenate([first, rolled[SUBLANES:]], axis=0)
        y = y + shifted * cw[conv_width - 1 - d:conv_width - d, :]
    tail_ref[...] = x[tb - SUBLANES:, :]

    yb = y.astype(BF16)
    r_parts, i_parts = [], []
    for hh in range(heads_per_block):
        yh = yb[:, hh * HEAD_DIM:(hh + 1) * HEAD_DIM]
        r_parts.append(jnp.dot(yh, wr_ref[hh], preferred_element_type=F32))
        i_parts.append(jnp.dot(yh, wi_ref[hh], preferred_element_type=F32))
    r = jax.nn.sigmoid(jnp.concatenate(r_parts, axis=1) + br_ref[...])
    i = jax.nn.sigmoid(jnp.concatenate(i_parts, axis=1) + bi_ref[...])
    log_a = (-LRU_C) * r * _softplus(-lam_ref[...])
    a = jnp.exp(log_a)
    a_scr[...] = a
    u_scr[...] = jnp.sqrt(-jnp.tanh(log_a) * (a * a + 1.0)) * (i * y)

    def group(gidx, h_prev):
        rows = pl.ds(pl.multiple_of(gidx * SUBLANES, SUBLANES), SUBLANES)
        a = a_scr[rows, :]
        u = u_scr[rows, :]
        for d in (1, 2, 4):
            keep = row8 >= d
            u = jnp.where(keep, a * pltpu.roll(u, d, 0) + u, u)
            a = jnp.where(keep, a * pltpu.roll(a, d, 0), a)
        h = a * h_prev + u
        u_scr[rows, :] = h
        return h[SUBLANES - 1:SUBLANES, :]

    h_last = lax.fori_loop(0, tb // SUBLANES, group, hlast_ref[0:1, :])
    hlast_ref[0:1, :] = h_last
    o_ref[...] = (u_scr[...] * _gelu_tanh(gate_ref[...].astype(F32))).astype(o_ref.dtype)


def lru_branch(proj, conv_w, conv_b, wr, br, wi, bi, lam, *, width, tb=512, cb=256):
    s = proj.shape[0]
    tb = min(tb, s)
    nc = width // cb
    hpb = cb // HEAD_DIM
    kw = conv_w.shape[0]
    vec = lambda a: a.reshape(1, width).astype(F32)
    vec_spec = pl.BlockSpec((1, cb), lambda c, t: (0, c))
    return pl.pallas_call(
        functools.partial(_lru_kernel, heads_per_block=hpb, conv_width=kw),
        grid=(nc, s // tb),
        in_specs=[pl.BlockSpec((tb, cb), lambda c, t: (t, c)),
                  pl.BlockSpec((tb, cb), lambda c, t: (t, nc + c)),
                  pl.BlockSpec((kw, cb), lambda c, t: (0, c)),
                  vec_spec,
                  pl.BlockSpec((hpb, HEAD_DIM, HEAD_DIM), lambda c, t: (c, 0, 0)),
                  vec_spec,
                  pl.BlockSpec((hpb, HEAD_DIM, HEAD_DIM), lambda c, t: (c, 0, 0)),
                  vec_spec, vec_spec],
        out_specs=pl.BlockSpec((tb, cb), lambda c, t: (t, c)),
        out_shape=jax.ShapeDtypeStruct((s, width), BF16),
        scratch_shapes=[pltpu.VMEM((SUBLANES, cb), F32), pltpu.VMEM((SUBLANES, cb), F32),
                        pltpu.VMEM((tb, cb), F32), pltpu.VMEM((tb, cb), F32)],
        compiler_params=_params("arbitrary", "arbitrary"),
        name="lru_branch",
    )(proj, proj, conv_w.astype(F32), vec(conv_b), wr.astype(BF16), vec(br), wi.astype(BF16),
      vec(bi), vec(lam))


def _stick_kernel(q_ref, k_ref, v_ref, o_ref, c_scr, acc_scr, *, tk, group):
    qi = pl.program_id(1)
    tq = q_ref.shape[0]
    n_diag = tq // tk
    scale = HEAD_DIM ** -0.5
    rr = lax.broadcasted_iota(jnp.int32, (tk, 2 * tk), 0)
    cc = lax.broadcasted_iota(jnp.int32, (tk, 2 * tk), 1)
    cum_mat = jnp.where((rr > cc) | (cc >= tk), 1.0, 0.0).astype(BF16)

    def block(j, g, c, causal_mask):
        rows = pl.ds(pl.multiple_of(j * tk, tk), tk)
        lanes = slice(g * HEAD_DIM, (g + 1) * HEAD_DIM)
        z = lax.dot_general(q_ref[:, lanes], k_ref[rows, lanes], NT_DIMS,
                            preferred_element_type=F32) * scale
        sp = _softplus(z)
        log_keep = -sp
        log_beta = z - sp
        if causal_mask is not None:
            log_keep = jnp.where(causal_mask, log_keep, 0.0)
        hi, lo = _split_bf16(log_keep)
        cum = (jnp.dot(hi, cum_mat, preferred_element_type=F32)
               + jnp.dot(lo, cum_mat, preferred_element_type=F32))
        w = jnp.exp(log_beta + cum[:, :tk] + c)
        if causal_mask is not None:
            w = jnp.where(causal_mask, w, 0.0)
        pv = jnp.dot(w.astype(BF16), v_ref[rows, lanes], preferred_element_type=F32)
        return pv, c + cum[:, tk:]

    qrow = lax.broadcasted_iota(jnp.int32, (tq, tk), 0)
    kcol = lax.broadcasted_iota(jnp.int32, (tq, tk), 1)
    cs = [jnp.zeros((tq, tk), F32) for _ in range(group)]
    accs = [jnp.zeros((tq, HEAD_DIM), F32) for _ in range(group)]
    for dj in reversed(range(n_diag)):
        mask = (kcol + dj * tk) < qrow
        for g in range(group):
            pv, cs[g] = block(qi * n_diag + dj, g, cs[g], mask)
            accs[g] = accs[g] + pv
    cmax = jnp.max(cs[0])
    for g in range(group):
        c_scr[g] = cs[g]
        acc_scr[g] = accs[g]
        if g:
            cmax = jnp.maximum(cmax, jnp.max(cs[g]))

    def cond(carry):
        j, cmax = carry
        return jnp.logical_and(j >= 0, cmax > EXP_UNDERFLOW)

    def body(carry):
        j, _ = carry
        cmax = None
        for g in range(group):
            pv, c = block(j, g, c_scr[g], None)
            acc_scr[g] += pv
            c_scr[g] = c
            m = jnp.max(c)
            cmax = m if cmax is None else jnp.maximum(cmax, m)
        return j - 1, cmax

    lax.while_loop(cond, body, (qi * n_diag - 1, cmax))
    for g in range(group):
        o_ref[:, g * HEAD_DIM:(g + 1) * HEAD_DIM] = acc_scr[g].astype(o_ref.dtype)


def stick_breaking(proj, *, heads, q_col, k_col, v_col, tq=256, tk=256, group=4):
    s = proj.shape[0]
    tq = min(tq, s)
    gw = group * HEAD_DIM
    assert heads % group == 0 and q_col % group == 0 and k_col % group == 0 and v_col % group == 0
    return pl.pallas_call(
        functools.partial(_stick_kernel, tk=tk, group=group),
        grid=(heads // group, s // tq),
        in_specs=[pl.BlockSpec((tq, gw), lambda h, i: (i, q_col // group + h)),
                  pl.BlockSpec((s, gw), lambda h, i: (0, k_col // group + h)),
                  pl.BlockSpec((s, gw), lambda h, i: (0, v_col // group + h))],
        out_specs=pl.BlockSpec((tq, gw), lambda h, i: (i, h)),
        out_shape=jax.ShapeDtypeStruct((s, heads * HEAD_DIM), BF16),
        scratch_shapes=[pltpu.VMEM((group, tq, tk), F32), pltpu.VMEM((group, tq, HEAD_DIM), F32)],
        compiler_params=_params("arbitrary", "arbitrary"),
        name="stick_breaking",
    )(proj, proj, proj)


def _gla_kernel(q_ref, k_ref, v_ref, gate_ref, lr_ref, wlr_ref, bg_ref, gn_ref, o_ref, state_scr, *, group):
    t = pl.program_id(1)
    tc = q_ref.shape[0]

    @pl.when(t == 0)
    def _():
        state_scr[...] = jnp.zeros_like(state_scr)

    rr = lax.broadcasted_iota(jnp.int32, (tc, tc), 0)
    cc = lax.broadcasted_iota(jnp.int32, (tc, tc), 1)
    same = (rr // GLA_CHUNK) == (cc // GLA_CHUNK)
    cum_mat = jnp.where(same & (cc <= rr), 1.0, 0.0).astype(BF16)
    tot_mat = jnp.where(same, 1.0, 0.0).astype(BF16)
    tril = (lax.broadcasted_iota(jnp.int32, (GLA_CHUNK, GLA_CHUNK), 1)
            <= lax.broadcasted_iota(jnp.int32, (GLA_CHUNK, GLA_CHUNK), 0))
    lr = lr_ref[...]

    for g in range(group):
        lanes = slice(g * HEAD_DIM, (g + 1) * HEAD_DIM)
        pre = jnp.dot(lr, wlr_ref[:, lanes], preferred_element_type=F32) + bg_ref[:, lanes]
        log_g = -_softplus(-pre) * (1.0 / GLA_GATE_TEMP)
        hi, lo = _split_bf16(log_g)
        b = jnp.dot(cum_mat, hi, preferred_element_type=F32) + jnp.dot(cum_mat, lo, preferred_element_type=F32)
        b_last = jnp.dot(tot_mat, hi, preferred_element_type=F32) + jnp.dot(tot_mat, lo, preferred_element_type=F32)

        q = q_ref[:, lanes].astype(F32) * (HEAD_DIM ** -0.5)
        k = k_ref[:, lanes].astype(F32)
        q_dec = (q * jnp.exp(b)).astype(BF16)
        k_in = (k * jnp.exp(-b)).astype(BF16)
        k_state = (k * jnp.exp(b_last - b)).astype(BF16)
        v = v_ref[:, lanes]

        state_t = state_scr[g]
        outs = []
        for ci in range(tc // GLA_CHUNK):
            sl = slice(ci * GLA_CHUNK, (ci + 1) * GLA_CHUNK)
            p = lax.dot_general(q_dec[sl], k_in[sl], NT_DIMS, preferred_element_type=F32)
            p = jnp.where(tril, p, 0.0)
            o = jnp.dot(p.astype(BF16), v[sl], preferred_element_type=F32)
            o = o + lax.dot_general(q_dec[sl], state_t.astype(BF16), NT_DIMS, preferred_element_type=F32)
            decay = jnp.exp(b_last[ci * GLA_CHUNK:ci * GLA_CHUNK + 1, :])
            state_t = state_t * decay + lax.dot_general(v[sl], k_state[sl], TN_DIMS,
                                                        preferred_element_type=F32)
            outs.append(o)
        state_scr[g] = state_t
        o = jnp.concatenate(outs, axis=0)
        o = o * lax.rsqrt(jnp.mean(o * o, axis=-1, keepdims=True) + EPS) * gn_ref[...]
        gate = gate_ref[:, lanes].astype(F32)
        o_ref[:, lanes] = (o * (gate * jax.nn.sigmoid(gate))).astype(o_ref.dtype)


def gla_branch(proj, lr, w_lr, b_gate, gn, *, heads, q_col, k_col, v_col, gate_col, tc=256, group=4):
    s = proj.shape[0]
    tc = min(tc, s)
    gw = group * HEAD_DIM
    assert heads % group == 0 and all(c % group == 0 for c in (q_col, k_col, v_col, gate_col))
    blk = lambda col: pl.BlockSpec((tc, gw), lambda h, t: (t, col // group + h))
    return pl.pallas_call(
        functools.partial(_gla_kernel, group=group),
        grid=(heads // group, s // tc),
        in_specs=[blk(q_col), blk(k_col), blk(v_col), blk(gate_col),
                  pl.BlockSpec((tc, LANES), lambda h, t: (t, 0)),
                  pl.BlockSpec((LANES, gw), lambda h, t: (0, h)),
                  pl.BlockSpec((1, gw), lambda h, t: (0, h)),
                  pl.BlockSpec((1, HEAD_DIM), lambda h, t: (0, 0))],
        out_specs=pl.BlockSpec((tc, gw), lambda h, t: (t, h)),
        out_shape=jax.ShapeDtypeStruct((s, heads * HEAD_DIM), BF16),
        scratch_shapes=[pltpu.VMEM((group, HEAD_DIM, HEAD_DIM), F32)],
        compiler_params=_params("arbitrary", "arbitrary"),
        name="gla_branch",
    )(proj, proj, proj, proj, lr, w_lr, b_gate.reshape(1, -1).astype(F32), gn.reshape(1, -1).astype(F32))


def _merge_kernel(ya_ref, yb_ref, yc_ref, wa_ref, wb_ref, wc_ref, ga_ref, gb_ref, gc_ref, o_ref):
    def term(y_ref, w_ref, g_ref):
        return (jax.nn.sigmoid(g_ref[...].astype(F32))
                * jnp.dot(y_ref[...], w_ref[...], preferred_element_type=F32))
    o_ref[...] = (term(ya_ref, wa_ref, ga_ref) + term(yb_ref, wb_ref, gb_ref)
                  + term(yc_ref, wc_ref, gc_ref)).astype(o_ref.dtype)


def merge_branches(ya, yb, yc, wa, wb, wc, gates, *, bm=512, bn=1024):
    s, w = ya.shape
    d = wa.shape[1]
    bm, bn = min(bm, s), min(bn, d)
    nb = d // bn
    y_spec = pl.BlockSpec((bm, w), lambda i, j: (i, 0))
    w_spec = pl.BlockSpec((w, bn), lambda i, j: (0, j))
    g_spec = lambda b: pl.BlockSpec((bm, bn), lambda i, j: (i, b * nb + j))
    return pl.pallas_call(
        _merge_kernel,
        grid=(s // bm, nb),
        in_specs=[y_spec, y_spec, y_spec, w_spec, w_spec, w_spec, g_spec(0), g_spec(1), g_spec(2)],
        out_specs=pl.BlockSpec((bm, bn), lambda i, j: (i, j)),
        out_shape=jax.ShapeDtypeStruct((s, d), BF16),
        compiler_params=_params("arbitrary", "arbitrary"),
        name="merge_branches",
    )(ya, yb, yc, wa, wb, wc, gates, gates, gates)


def _xattn_kernel(q_ref, k_ref, v_ref, o_ref, *, heads):
    hd = q_ref.shape[1] // heads
    scale = hd ** -0.5
    outs = []
    for h in range(heads):
        sl = slice(h * hd, (h + 1) * hd)
        s = lax.dot_general(q_ref[:, sl], k_ref[:, sl], NT_DIMS, preferred_element_type=F32) * scale
        s = s - jnp.max(s, axis=-1, keepdims=True)
        e = jnp.exp(s)
        p = e / jnp.sum(e, axis=-1, keepdims=True)
        outs.append(jnp.dot(p.astype(BF16), v_ref[:, sl], preferred_element_type=F32))
    o_ref[...] = jnp.concatenate(outs, axis=1).astype(o_ref.dtype)


def cross_attention(q, k, v, *, heads, tq=512):
    s, w = q.shape
    m = k.shape[0]
    tq = min(tq, s)
    return pl.pallas_call(
        functools.partial(_xattn_kernel, heads=heads),
        grid=(s // tq,),
        in_specs=[pl.BlockSpec((tq, w), lambda i: (i, 0)),
                  pl.BlockSpec((m, w), lambda i: (0, 0)),
                  pl.BlockSpec((m, w), lambda i: (0, 0))],
        out_specs=pl.BlockSpec((tq, w), lambda i: (i, 0)),
        out_shape=jax.ShapeDtypeStruct((s, w), BF16),
        compiler_params=_params("arbitrary"),
        name="cross_attention",
    )(q, k, v)


def _topk_rows(s, k):
    n = s.shape[0]
    rows = lax.broadcasted_iota(jnp.int32, s.shape, 0).astype(F32)
    rank = jnp.full(s.shape, NOT_RANKED, F32)
    vals, idxs = [], []
    for it in range(k):
        m = jnp.max(s, axis=0, keepdims=True)
        idx = jnp.min(jnp.where(s == m, rows, float(n)), axis=0, keepdims=True)
        hit = rows == idx
        rank = jnp.where(hit, float(it), rank)
        s = jnp.where(hit, -jnp.inf, s)
        vals.append(m)
        idxs.append(idx)
    return vals, idxs, rank


def _peer_select_kernel(q_ref, keys_ref, e0_ref, c0_ref, e1_ref, r1_ref):
    qd = keys_ref.shape[3]
    part = []
    for p in range(2):
        qh, ql = _split_bf16(q_ref[:, p * qd:(p + 1) * qd])
        kh, kl = _split_bf16(keys_ref[0, p])
        s = (lax.dot_general(kh, qh, NT_DIMS, preferred_element_type=F32)
             + lax.dot_general(kh, ql, NT_DIMS, preferred_element_type=F32)
             + lax.dot_general(kl, qh, NT_DIMS, preferred_element_type=F32))
        vals, _, rank = _topk_rows(s, PEER_TOPK)
        part.append((s, vals, rank))
    (s0, v0, rank0), (s1, v1, rank1) = part
    pairs = [(a, b) for a in range(PEER_TOPK) for b in range(PEER_TOPK) if (a + 1) * (b + 1) <= PEER_TOPK]
    cand = jnp.concatenate([v0[a] + v1[b] for a, b in pairs], axis=0)
    pad = (-len(pairs)) % SUBLANES
    if pad:
        cand = jnp.concatenate([cand, jnp.full((pad, cand.shape[1]), -jnp.inf, F32)], axis=0)
    best, _, rank_c = _topk_rows(cand, PEER_TOPK)
    z = jnp.zeros_like(best[0])
    for b in best:
        z = z + jnp.exp(b - best[0])
    chosen = jnp.where(rank_c < NOT_RANKED, 1.0, 0.0)
    c0 = jnp.zeros_like(s0)
    crow = lax.broadcasted_iota(jnp.int32, chosen.shape, 0)
    for a in range(PEER_TOPK):
        rows_a = [r for r, (aa, _) in enumerate(pairs) if aa == a]
        in_a = (crow >= rows_a[0]) & (crow <= rows_a[-1])
        cnt = jnp.sum(jnp.where(in_a, chosen, 0.0), axis=0, keepdims=True)
        c0 = jnp.where(rank0 == float(a), cnt, c0)
    e0_ref[...] = jnp.exp(s0 - v0[0])
    c0_ref[...] = c0
    e1_ref[...] = (jnp.exp(s1 - v1[0]) / z).astype(e1_ref.dtype)
    r1_ref[...] = rank1.astype(r1_ref.dtype)


def peer_select(q, keys, *, tb=512):
    t = q.shape[0]
    h, _, nk, qd = keys.shape
    tb = min(tb, t)
    tab = jax.ShapeDtypeStruct((h * nk, t), F32)
    tab_bf16 = jax.ShapeDtypeStruct((h * nk, t), BF16)
    tab_spec = pl.BlockSpec((nk, tb), lambda i, hh: (hh, i))
    return pl.pallas_call(
        _peer_select_kernel,
        grid=(t // tb, h),
        in_specs=[pl.BlockSpec((tb, 2 * qd), lambda i, hh: (i, hh)),
                  pl.BlockSpec((1, 2, nk, qd), lambda i, hh: (hh, 0, 0, 0))],
        out_specs=[tab_spec] * 4,
        out_shape=[tab, tab, tab_bf16, tab_bf16],
        compiler_params=_params("arbitrary", "arbitrary"),
        name="peer_select",
    )(q, keys)


def _peer_expert_kernel(h_ref, down_ref, up_ref, e0_ref, c0_ref, e1_ref, r1_ref, o_ref, *, heads, nk):
    e = pl.program_id(1)
    eb = down_ref.shape[0]

    @pl.when(e == 0)
    def _():
        o_ref[...] = jnp.zeros_like(o_ref)

    act = lax.dot_general(down_ref[...], h_ref[...], NT_DIMS, preferred_element_type=F32)
    act = _gelu_tanh(act).astype(BF16)
    w_rows = []
    for il in range(eb // nk):
        i = e * (eb // nk) + il
        w = None
        for h in range(heads):
            row = pl.ds(h * nk + i, 1)
            blk = slice(h * nk, (h + 1) * nk)
            count = c0_ref[row, :].astype(BF16)
            gate0 = e0_ref[row, :].astype(BF16)
            contrib = jnp.where(r1_ref[blk, :] < count, gate0 * e1_ref[blk, :], jnp.zeros((), BF16))
            w = contrib if w is None else w + contrib
        w_rows.append(w)
    xt = jnp.concatenate(w_rows, axis=0) * act
    o_ref[...] += lax.dot_general(xt, up_ref[...], TN_DIMS, preferred_element_type=F32)


def peer_experts(h, down, up, tables, *, layer, heads, nk, tb=512, eb=512):
    t, d = h.shape
    n_exp = down.shape[1]
    tb, eb = min(tb, t), min(eb, n_exp)
    once = dict(pipeline_mode=pl.Buffered(1))
    tab_spec = pl.BlockSpec((heads * nk, tb), lambda i, e: (0, i), **once)
    return pl.pallas_call(
        functools.partial(_peer_expert_kernel, heads=heads, nk=nk),
        grid=(t // tb, n_exp // eb),
        in_specs=[pl.BlockSpec((tb, d), lambda i, e: (i, 0), **once),
                  pl.BlockSpec((None, eb, d), lambda i, e: (layer, e, 0)),
                  pl.BlockSpec((None, eb, d), lambda i, e: (layer, e, 0))] + [tab_spec] * 4,
        out_specs=pl.BlockSpec((tb, d), lambda i, e: (i, 0)),
        out_shape=jax.ShapeDtypeStruct((t, d), F32),
        compiler_params=_params("arbitrary", "arbitrary"),
        name="peer_experts",
    )(h, down, up, *tables)


def kernel(x, mem, norm_mix, w_in, conv_w, conv_b, lru_wr, lru_br, lru_wi, lru_bi, lru_lambda, gla_w_lr, gla_b_gate, gla_norm, w_pa, w_pb, w_pc, w_out, norm_cross, norm_mem, x_wq, x_wk, x_wv, x_wo, norm_ffn, peer_wq, peer_keys, peer_down, peer_up, norm_final):
    depth = w_in.shape[0]
    batch, seq, d = x.shape
    assert batch == 1
    width = conv_w.shape[2]
    heads = width // HEAD_DIM
    rank = gla_w_lr.shape[1]
    n_main = 9 * width
    col = lambda k: k * heads

    xs = x.reshape(seq, d)
    mem2 = mem.reshape(mem.shape[1], d)
    w_in_t = jnp.swapaxes(w_in, 1, 2).astype(BF16)
    peer_down_bf16 = peer_down.astype(BF16)
    peer_up_bf16 = peer_up.astype(BF16)
    deltas = []
    for l in range(depth):
        w_lr = jnp.pad(w_in_t[l, n_main:n_main + rank], ((0, LANES - rank), (0, 0)))
        w_gates = w_in_t[l, n_main + rank:]

        if l == 0:
            h = resnorm(xs, [], norm_mix[l], emit_x=False, out_dtype=BF16)
        else:
            xs, h = resnorm(xs, deltas, norm_mix[l], emit_x=True, out_dtype=BF16)
        proj = matmul(h, w_in_t, BF16, transposed=True, layer=l, n=n_main)
        lr = matmul(h, w_lr, BF16, bn=LANES, transposed=True)
        gates = matmul(h, w_gates, BF16, transposed=True)

        ya = lru_branch(proj, conv_w[l], conv_b[l], lru_wr[l], lru_br[l], lru_wi[l], lru_bi[l],
                        lru_lambda[l], width=width)
        yb = stick_breaking(proj, heads=heads, q_col=col(2), k_col=col(3), v_col=col(4))
        w_lr_gate = jnp.pad(gla_w_lr[l], ((0, LANES - rank), (0, 0))).astype(BF16)
        yc = gla_branch(proj, lr, w_lr_gate, gla_b_gate[l], gla_norm[l], heads=heads,
                        q_col=col(5), k_col=col(6), v_col=col(7), gate_col=col(8))
        mixed = merge_branches(ya, yb, yc, w_pa[l].astype(BF16), w_pb[l].astype(BF16),
                               w_pc[l].astype(BF16), gates)
        delta = matmul(mixed, w_out[l].astype(BF16), F32)

        xs, h = resnorm(xs, [delta], norm_cross[l], emit_x=True, out_dtype=BF16)
        m = resnorm(mem2, [], norm_mem[l], emit_x=False, out_dtype=BF16)
        q = matmul(h, x_wq[l].astype(BF16), BF16)
        k = matmul(m, x_wk[l].astype(BF16), BF16)
        v = matmul(m, x_wv[l].astype(BF16), BF16)
        o = cross_attention(q, k, v, heads=X_HEADS)
        delta = matmul(o, x_wo[l].astype(BF16), F32)

        xs, h = resnorm(xs, [delta], norm_ffn[l], emit_x=True, out_dtype=BF16)
        pq = matmul(h, peer_wq[l].astype(BF16), F32)
        tables = peer_select(pq, peer_keys[l])
        delta = peer_experts(h, peer_down_bf16, peer_up_bf16, tables, layer=l,
                             heads=peer_keys.shape[1], nk=peer_keys.shape[3])
        deltas = [delta]

    out = resnorm(xs, deltas, norm_final, emit_x=False, out_dtype=F32)
    return out.reshape(batch, seq, d)
```

```python
import functools
import math

import jax
import jax.numpy as jnp
from jax import lax
from jax.experimental import pallas as pl
from jax.experimental.pallas import tpu as pltpu

F32 = jnp.float32
BF16 = jnp.bfloat16

LANES = 128
SUBLANES = 8
VMEM_LIMIT_BYTES = 56 * 1024 * 1024

EPS = 1e-6
HEAD_DIM = 128
LRU_C = 8.0
GLA_CHUNK = 64
GLA_GATE_TEMP = 16.0
X_HEADS = 4
PEER_TOPK = 16
EXP_UNDERFLOW = -105.0
NOT_RANKED = 4096.0

NT_DIMS = (((1,), (1,)), ((), ()))
TN_DIMS = (((0,), (0,)), ((), ()))


def _params(*semantics, flags=None):
    return pltpu.CompilerParams(dimension_semantics=semantics, vmem_limit_bytes=VMEM_LIMIT_BYTES,
                                flags=flags)


def _gelu_tanh(x):
    return 0.5 * x * (1.0 + jnp.tanh(math.sqrt(2.0 / math.pi) * (x + 0.044715 * (x * x * x))))


def _softplus(x):
    return jnp.maximum(x, 0.0) + jnp.log1p(jnp.exp(-jnp.abs(x)))


def _split_bf16(x):
    hi = x.astype(BF16)
    lo = (x - hi.astype(F32)).astype(BF16)
    return hi, lo


def _resnorm_kernel(*refs, n_delta, emit_x):
    x_ref = refs[0]
    d_refs = refs[1:1 + n_delta]
    g_ref = refs[1 + n_delta]
    outs = refs[2 + n_delta:]
    x = x_ref[...]
    for d in d_refs:
        x = x + d[...].astype(F32)
    y = x * lax.rsqrt(jnp.mean(x * x, axis=-1, keepdims=True) + EPS) * g_ref[...]
    if emit_x:
        outs[0][...] = x
        outs[1][...] = y.astype(outs[1].dtype)
    else:
        outs[0][...] = y.astype(outs[0].dtype)


def resnorm(x, deltas, g, *, emit_x, out_dtype, block_rows=256):
    rows, d = x.shape
    block_rows = min(block_rows, rows)
    row_spec = pl.BlockSpec((block_rows, d), lambda i: (i, 0))
    out_shape = [jax.ShapeDtypeStruct((rows, d), out_dtype)]
    out_specs = [row_spec]
    if emit_x:
        out_shape = [jax.ShapeDtypeStruct((rows, d), F32)] + out_shape
        out_specs = [row_spec] + out_specs
    res = pl.pallas_call(
        functools.partial(_resnorm_kernel, n_delta=len(deltas), emit_x=emit_x),
        grid=(rows // block_rows,),
        in_specs=[row_spec] * (1 + len(deltas)) + [pl.BlockSpec((1, d), lambda i: (0, 0))],
        out_specs=out_specs,
        out_shape=out_shape,
        compiler_params=_params("arbitrary"),
        name="resnorm",
    )(x, *deltas, g.reshape(1, d).astype(F32))
    return res if emit_x else res[0]


def _mm_kernel(a_ref, w_ref, o_ref):
    o_ref[...] = jnp.dot(a_ref[...], w_ref[...], preferred_element_type=F32).astype(o_ref.dtype)


def _mm_nt_kernel(a_ref, wt_ref, o_ref):
    o_ref[...] = lax.dot_general(a_ref[...], wt_ref[...], NT_DIMS,
                                 preferred_element_type=F32).astype(o_ref.dtype)


def matmul(a, w, out_dtype, *, bm=1024, bn=1024, transposed=False, layer=None, n=None):
    m, k = a.shape
    if layer is not None:
        assert transposed and w.ndim == 3 and n % min(bn, n) == 0
    else:
        n = w.shape[0] if transposed else w.shape[1]
    bm, bn = min(bm, m), min(bn, n)
    if layer is not None:
        w_spec = pl.BlockSpec((None, bn, k), lambda i, j: (layer, j, 0))
    elif transposed:
        w_spec = pl.BlockSpec((bn, k), lambda i, j: (j, 0))
    else:
        w_spec = pl.BlockSpec((k, bn), lambda i, j: (0, j))
    return pl.pallas_call(
        _mm_nt_kernel if transposed else _mm_kernel,
        grid=(m // bm, n // bn),
        in_specs=[pl.BlockSpec((bm, k), lambda i, j: (i, 0)), w_spec],
        out_specs=pl.BlockSpec((bm, bn), lambda i, j: (i, j)),
        out_shape=jax.ShapeDtypeStruct((m, n), out_dtype),
        compiler_params=_params("arbitrary", "arbitrary"),
        name="matmul",
    )(a, w)


def _lru_kernel(ax_ref, gate_ref, cw_ref, cb_ref, wr_ref, br_ref, wi_ref, bi_ref, lam_ref,
                o_ref, tail_ref, hlast_ref, a_scr, u_scr, *, heads_per_block, conv_width):
    t = pl.program_id(1)
    tb, cb = ax_ref.shape

    @pl.when(t == 0)
    def _():
        tail_ref[...] = jnp.zeros_like(tail_ref)
        hlast_ref[...] = jnp.zeros_like(hlast_ref)

    x = ax_ref[...].astype(F32)
    prev = tail_ref[...]
    row8 = lax.broadcasted_iota(jnp.int32, (SUBLANES, cb), 0)
    cw = cw_ref[...]
    y = x * cw[conv_width - 1:conv_width, :] + cb_ref[...]
    for d in range(1, conv_width):
        rolled = pltpu.roll(x, d, 0)
        first = jnp.where(row8 < d, pltpu.roll(prev, d, 0), rolled[:SUBLANES])
        shifted = jnp.concatenate([first, rolled[SUBLANES:]], axis=0)
        y = y + shifted * cw[conv_width - 1 - d:conv_width - d, :]
    tail_ref[...] = x[tb - SUBLANES:, :]

    yb = y.astype(BF16)
    r_parts, i_parts = [], []
    for hh in range(heads_per_block):
        yh = yb[:, hh * HEAD_DIM:(hh + 1) * HEAD_DIM]
        r_parts.append(jnp.dot(yh, wr_ref[hh], preferred_element_type=F32))
        i_parts.append(jnp.dot(yh, wi_ref[hh], preferred_element_type=F32))
    r = jax.nn.sigmoid(jnp.concatenate(r_parts, axis=1) + br_ref[...])
    i = jax.nn.sigmoid(jnp.concatenate(i_parts, axis=1) + bi_ref[...])
    log_a = (-LRU_C) * r * _softplus(-lam_ref[...])
    a = jnp.exp(log_a)
    a_scr[...] = a
    u_scr[...] = jnp.sqrt(-jnp.tanh(log_a) * (a * a + 1.0)) * (i * y)

    def group(gidx, h_prev):
        rows = pl.ds(pl.multiple_of(gidx * SUBLANES, SUBLANES), SUBLANES)
        a = a_scr[rows, :]
        u = u_scr[rows, :]
        for d in (1, 2, 4):
            keep = row8 >= d
            u = jnp.where(keep, a * pltpu.roll(u, d, 0) + u, u)
            a = jnp.where(keep, a * pltpu.roll(a, d, 0), a)
        h = a * h_prev + u
        u_scr[rows, :] = h
        return h[SUBLANES - 1:SUBLANES, :]

    h_last = lax.fori_loop(0, tb // SUBLANES, group, hlast_ref[0:1, :])
    hlast_ref[0:1, :] = h_last
    o_ref[...] = (u_scr[...] * _gelu_tanh(gate_ref[...].astype(F32))).astype(o_ref.dtype)


def lru_branch(proj, conv_w, conv_b, wr, br, wi, bi, lam, *, width, tb=512, cb=256):
    s = proj.shape[0]
    tb = min(tb, s)
    nc = width // cb
    hpb = cb // HEAD_DIM
    kw = conv_w.shape[0]
    vec = lambda a: a.reshape(1, width).astype(F32)
    vec_spec = pl.BlockSpec((1, cb), lambda c, t: (0, c))
    return pl.pallas_call(
        functools.partial(_lru_kernel, heads_per_block=hpb, conv_width=kw),
        grid=(nc, s // tb),
        in_specs=[pl.BlockSpec((tb, cb), lambda c, t: (t, c)),
                  pl.BlockSpec((tb, cb), lambda c, t: (t, nc + c)),
                  pl.BlockSpec((kw, cb), lambda c, t: (0, c)),
                  vec_spec,
                  pl.BlockSpec((hpb, HEAD_DIM, HEAD_DIM), lambda c, t: (c, 0, 0)),
                  vec_spec,
                  pl.BlockSpec((hpb, HEAD_DIM, HEAD_DIM), lambda c, t: (c, 0, 0)),
                  vec_spec, vec_spec],
        out_specs=pl.BlockSpec((tb, cb), lambda c, t: (t, c)),
        out_shape=jax.ShapeDtypeStruct((s, width), BF16),
        scratch_shapes=[pltpu.VMEM((SUBLANES, cb), F32), pltpu.VMEM((SUBLANES, cb), F32),
                        pltpu.VMEM((tb, cb), F32), pltpu.VMEM((tb, cb), F32)],
        compiler_params=_params("arbitrary", "arbitrary"),
        name="lru_branch",
    )(proj, proj, conv_w.astype(F32), vec(conv_b), wr.astype(BF16), vec(br), wi.astype(BF16),
      vec(bi), vec(lam))


def _stick_kernel(q_ref, k_ref, v_ref, o_ref, c_scr, acc_scr, *, tk, group):
    qi = pl.program_id(1)
    tq = q_ref.shape[0]
    n_diag = tq // tk
    scale = HEAD_DIM ** -0.5
    rr = lax.broadcasted_iota(jnp.int32, (tk, 2 * tk), 0)
    cc = lax.broadcasted_iota(jnp.int32, (tk, 2 * tk), 1)
    cum_mat = jnp.where((rr > cc) | (cc >= tk), 1.0, 0.0).astype(BF16)

    def block(j, g, c, causal_mask):
        rows = pl.ds(pl.multiple_of(j * tk, tk), tk)
        lanes = slice(g * HEAD_DIM, (g + 1) * HEAD_DIM)
        z = lax.dot_general(q_ref[:, lanes], k_ref[rows, lanes], NT_DIMS,
                            preferred_element_type=F32) * scale
        sp = _softplus(z)
        log_keep = -sp
        log_beta = z - sp
        if causal_mask is not None:
            log_keep = jnp.where(causal_mask, log_keep, 0.0)
        hi, lo = _split_bf16(log_keep)
        cum = (jnp.dot(hi, cum_mat, preferred_element_type=F32)
               + jnp.dot(lo, cum_mat, preferred_element_type=F32))
        w = jnp.exp(log_beta + cum[:, :tk] + c)
        if causal_mask is not None:
            w = jnp.where(causal_mask, w, 0.0)
        pv = jnp.dot(w.astype(BF16), v_ref[rows, lanes], preferred_element_type=F32)
        return pv, c + cum[:, tk:]

    qrow = lax.broadcasted_iota(jnp.int32, (tq, tk), 0)
    kcol = lax.broadcasted_iota(jnp.int32, (tq, tk), 1)
    cs = [jnp.zeros((tq, tk), F32) for _ in range(group)]
    accs = [jnp.zeros((tq, HEAD_DIM), F32) for _ in range(group)]
    for dj in reversed(range(n_diag)):
        mask = (kcol + dj * tk) < qrow
        for g in range(group):
            pv, cs[g] = block(qi * n_diag + dj, g, cs[g], mask)
            accs[g] = accs[g] + pv
    cmax = jnp.max(cs[0])
    for g in range(group):
        c_scr[g] = cs[g]
        acc_scr[g] = accs[g]
        if g:
            cmax = jnp.maximum(cmax, jnp.max(cs[g]))

    def cond(carry):
        j, cmax = carry
        return jnp.logical_and(j >= 0, cmax > EXP_UNDERFLOW)

    def body(carry):
        j, _ = carry
        cmax = None
        for g in range(group):
            pv, c = block(j, g, c_scr[g], None)
            acc_scr[g] += pv
            c_scr[g] = c
            m = jnp.max(c)
            cmax = m if cmax is None else jnp.maximum(cmax, m)
        return j - 1, cmax

    lax.while_loop(cond, body, (qi * n_diag - 1, cmax))
    for g in range(group):
        o_ref[:, g * HEAD_DIM:(g + 1) * HEAD_DIM] = acc_scr[g].astype(o_ref.dtype)


def stick_breaking(proj, *, heads, q_col, k_col, v_col, tq=256, tk=256, group=4):
    s = proj.shape[0]
    tq = min(tq, s)
    gw = group * HEAD_DIM
    assert heads % group == 0 and q_col % group == 0 and k_col % group == 0 and v_col % group == 0
    return pl.pallas_call(
        functools.partial(_stick_kernel, tk=tk, group=group),
        grid=(heads // group, s // tq),
        in_specs=[pl.BlockSpec((tq, gw), lambda h, i: (i, q_col // group + h)),
                  pl.BlockSpec((s, gw), lambda h, i: (0, k_col // group + h)),
                  pl.BlockSpec((s, gw), lambda h, i: (0, v_col // group + h))],
        out_specs=pl.BlockSpec((tq, gw), lambda h, i: (i, h)),
        out_shape=jax.ShapeDtypeStruct((s, heads * HEAD_DIM), BF16),
        scratch_shapes=[pltpu.VMEM((group, tq, tk), F32), pltpu.VMEM((group, tq, HEAD_DIM), F32)],
        compiler_params=_params("arbitrary", "arbitrary"),
        name="stick_breaking",
    )(proj, proj, proj)


def _gla_kernel(q_ref, k_ref, v_ref, gate_ref, lr_ref, wlr_ref, bg_ref, gn_ref, o_ref, state_scr, *, group):
    t = pl.program_id(1)
    tc = q_ref.shape[0]

    @pl.when(t == 0)
    def _():
        state_scr[...] = jnp.zeros_like(state_scr)

    rr = lax.broadcasted_iota(jnp.int32, (tc, tc), 0)
    cc = lax.broadcasted_iota(jnp.int32, (tc, tc), 1)
    same = (rr // GLA_CHUNK) == (cc // GLA_CHUNK)
    cum_mat = jnp.where(same & (cc <= rr), 1.0, 0.0).astype(BF16)
    tot_mat = jnp.where(same, 1.0, 0.0).astype(BF16)
    tril = (lax.broadcasted_iota(jnp.int32, (GLA_CHUNK, GLA_CHUNK), 1)
            <= lax.broadcasted_iota(jnp.int32, (GLA_CHUNK, GLA_CHUNK), 0))
    lr = lr_ref[...]

    for g in range(group):
        lanes = slice(g * HEAD_DIM, (g + 1) * HEAD_DIM)
        pre = jnp.dot(lr, wlr_ref[:, lanes], preferred_element_type=F32) + bg_ref[:, lanes]
        log_g = -_softplus(-pre) * (1.0 / GLA_GATE_TEMP)
        hi, lo = _split_bf16(log_g)
        b = jnp.dot(cum_mat, hi, preferred_element_type=F32) + jnp.dot(cum_mat, lo, preferred_element_type=F32)
        b_last = jnp.dot(tot_mat, hi, preferred_element_type=F32) + jnp.dot(tot_mat, lo, preferred_element_type=F32)

        q = q_ref[:, lanes].astype(F32) * (HEAD_DIM ** -0.5)
        k = k_ref[:, lanes].astype(F32)
        q_dec = (q * jnp.exp(b)).astype(BF16)
        k_in = (k * jnp.exp(-b)).astype(BF16)
        k_state = (k * jnp.exp(b_last - b)).astype(BF16)
        v = v_ref[:, lanes]

        state_t = state_scr[g]
        outs = []
        for ci in range(tc // GLA_CHUNK):
            sl = slice(ci * GLA_CHUNK, (ci + 1) * GLA_CHUNK)
            p = lax.dot_general(q_dec[sl], k_in[sl], NT_DIMS, preferred_element_type=F32)
            p = jnp.where(tril, p, 0.0)
            o = jnp.dot(p.astype(BF16), v[sl], preferred_element_type=F32)
            o = o + lax.dot_general(q_dec[sl], state_t.astype(BF16), NT_DIMS, preferred_element_type=F32)
            decay = jnp.exp(b_last[ci * GLA_CHUNK:ci * GLA_CHUNK + 1, :])
            state_t = state_t * decay + lax.dot_general(v[sl], k_state[sl], TN_DIMS,
                                                        preferred_element_type=F32)
            outs.append(o)
        state_scr[g] = state_t
        o = jnp.concatenate(outs, axis=0)
        o = o * lax.rsqrt(jnp.mean(o * o, axis=-1, keepdims=True) + EPS) * gn_ref[...]
        gate = gate_ref[:, lanes].astype(F32)
        o_ref[:, lanes] = (o * (gate * jax.nn.sigmoid(gate))).astype(o_ref.dtype)


def gla_branch(proj, lr, w_lr, b_gate, gn, *, heads, q_col, k_col, v_col, gate_col, tc=256, group=4):
    s = proj.shape[0]
    tc = min(tc, s)
    gw = group * HEAD_DIM
    assert heads % group == 0 and all(c % group == 0 for c in (q_col, k_col, v_col, gate_col))
    blk = lambda col: pl.BlockSpec((tc, gw), lambda h, t: (t, col // group + h))
    return pl.pallas_call(
        functools.partial(_gla_kernel, group=group),
        grid=(heads // group, s // tc),
        in_specs=[blk(q_col), blk(k_col), blk(v_col), blk(gate_col),
                  pl.BlockSpec((tc, LANES), lambda h, t: (t, 0)),
                  pl.BlockSpec((LANES, gw), lambda h, t: (0, h)),
                  pl.BlockSpec((1, gw), lambda h, t: (0, h)),
                  pl.BlockSpec((1, HEAD_DIM), lambda h, t: (0, 0))],
        out_specs=pl.BlockSpec((tc, gw), lambda h, t: (t, h)),
        out_shape=jax.ShapeDtypeStruct((s, heads * HEAD_DIM), BF16),
        scratch_shapes=[pltpu.VMEM((group, HEAD_DIM, HEAD_DIM), F32)],
        compiler_params=_params("arbitrary", "arbitrary"),
        name="gla_branch",
    )(proj, proj, proj, proj, lr, w_lr, b_gate.reshape(1, -1).astype(F32), gn.reshape(1, -1).astype(F32))


def _merge_kernel(ya_ref, yb_ref, yc_ref, wa_ref, wb_ref, wc_ref, ga_ref, gb_ref, gc_ref, o_ref):
    def term(y_ref, w_ref, g_ref):
        return (jax.nn.sigmoid(g_ref[...].astype(F32))
                * jnp.dot(y_ref[...], w_ref[...], preferred_element_type=F32))
    o_ref[...] = (term(ya_ref, wa_ref, ga_ref) + term(yb_ref, wb_ref, gb_ref)
                  + term(yc_ref, wc_ref, gc_ref)).astype(o_ref.dtype)


def merge_branches(ya, yb, yc, wa, wb, wc, gates, *, bm=512, bn=1024):
    s, w = ya.shape
    d = wa.shape[1]
    bm, bn = min(bm, s), min(bn, d)
    nb = d // bn
    y_spec = pl.BlockSpec((bm, w), lambda i, j: (i, 0))
    w_spec = pl.BlockSpec((w, bn), lambda i, j: (0, j))
    g_spec = lambda b: pl.BlockSpec((bm, bn), lambda i, j: (i, b * nb + j))
    return pl.pallas_call(
        _merge_kernel,
        grid=(s // bm, nb),
        in_specs=[y_spec, y_spec, y_spec, w_spec, w_spec, w_spec, g_spec(0), g_spec(1), g_spec(2)],
        out_specs=pl.BlockSpec((bm, bn), lambda i, j: (i, j)),
        out_shape=jax.ShapeDtypeStruct((s, d), BF16),
        compiler_params=_params("arbitrary", "arbitrary"),
        name="merge_branches",
    )(ya, yb, yc, wa, wb, wc, gates, gates, gates)


def _xattn_kernel(q_ref, k_ref, v_ref, o_ref, *, heads):
    hd = q_ref.shape[1] // heads
    scale = hd ** -0.5
    outs = []
    for h in range(heads):
        sl = slice(h * hd, (h + 1) * hd)
        s = lax.dot_general(q_ref[:, sl], k_ref[:, sl], NT_DIMS, preferred_element_type=F32) * scale
        s = s - jnp.max(s, axis=-1, keepdims=True)
        e = jnp.exp(s)
        p = e / jnp.sum(e, axis=-1, keepdims=True)
        outs.append(jnp.dot(p.astype(BF16), v_ref[:, sl], preferred_element_type=F32))
    o_ref[...] = jnp.concatenate(outs, axis=1).astype(o_ref.dtype)


def cross_attention(q, k, v, *, heads, tq=512):
    s, w = q.shape
    m = k.shape[0]
    tq = min(tq, s)
    return pl.pallas_call(
        functools.partial(_xattn_kernel, heads=heads),
        grid=(s // tq,),
        in_specs=[pl.BlockSpec((tq, w), lambda i: (i, 0)),
                  pl.BlockSpec((m, w), lambda i: (0, 0)),
                  pl.BlockSpec((m, w), lambda i: (0, 0))],
        out_specs=pl.BlockSpec((tq, w), lambda i: (i, 0)),
        out_shape=jax.ShapeDtypeStruct((s, w), BF16),
        compiler_params=_params("arbitrary"),
        name="cross_attention",
    )(q, k, v)


def _topk_rows(s, k):
    n = s.shape[0]
    rows = lax.broadcasted_iota(jnp.int32, s.shape, 0).astype(F32)
    rank = jnp.full(s.shape, NOT_RANKED, F32)
    vals, idxs = [], []
    for it in range(k):
        m = jnp.max(s, axis=0, keepdims=True)
        idx = jnp.min(jnp.where(s == m, rows, float(n)), axis=0, keepdims=True)
        hit = rows == idx
        rank = jnp.where(hit, float(it), rank)
        s = jnp.where(hit, -jnp.inf, s)
        vals.append(m)
        idxs.append(idx)
    return vals, idxs, rank


def _peer_select_kernel(q_ref, keys_ref, e0_ref, c0_ref, e1_ref, r1_ref):
    qd = keys_ref.shape[3]
    part = []
    for p in range(2):
        qh, ql = _split_bf16(q_ref[:, p * qd:(p + 1) * qd])
        kh, kl = _split_bf16(keys_ref[0, p])
        s = (lax.dot_general(kh, qh, NT_DIMS, preferred_element_type=F32)
             + lax.dot_general(kh, ql, NT_DIMS, preferred_element_type=F32)
             + lax.dot_general(kl, qh, NT_DIMS, preferred_element_type=F32))
        vals, _, rank = _topk_rows(s, PEER_TOPK)
        part.append((s, vals, rank))
    (s0, v0, rank0), (s1, v1, rank1) = part
    pairs = [(a, b) for a in range(PEER_TOPK) for b in range(PEER_TOPK) if (a + 1) * (b + 1) <= PEER_TOPK]
    cand = jnp.concatenate([v0[a] + v1[b] for a, b in pairs], axis=0)
    pad = (-len(pairs)) % SUBLANES
    if pad:
        cand = jnp.concatenate([cand, jnp.full((pad, cand.shape[1]), -jnp.inf, F32)], axis=0)
    best, _, rank_c = _topk_rows(cand, PEER_TOPK)
    z = jnp.zeros_like(best[0])
    for b in best:
        z = z + jnp.exp(b - best[0])
    chosen = jnp.where(rank_c < NOT_RANKED, 1.0, 0.0)
    c0 = jnp.zeros_like(s0)
    crow = lax.broadcasted_iota(jnp.int32, chosen.shape, 0)
    for a in range(PEER_TOPK):
        rows_a = [r for r, (aa, _) in enumerate(pairs) if aa == a]
        in_a = (crow >= rows_a[0]) & (crow <= rows_a[-1])
        cnt = jnp.sum(jnp.where(in_a, chosen, 0.0), axis=0, keepdims=True)
        c0 = jnp.where(rank0 == float(a), cnt, c0)
    e0_ref[...] = jnp.exp(s0 - v0[0])
    c0_ref[...] = c0
    e1_ref[...] = (jnp.exp(s1 - v1[0]) / z).astype(e1_ref.dtype)
    r1_ref[...] = rank1.astype(r1_ref.dtype)


def peer_select(q, keys, *, tb=512):
    t = q.shape[0]
    h, _, nk, qd = keys.shape
    tb = min(tb, t)
    tab = jax.ShapeDtypeStruct((h * nk, t), F32)
    tab_bf16 = jax.ShapeDtypeStruct((h * nk, t), BF16)
    tab_spec = pl.BlockSpec((nk, tb), lambda i, hh: (hh, i))
    return pl.pallas_call(
        _peer_select_kernel,
        grid=(t // tb, h),
        in_specs=[pl.BlockSpec((tb, 2 * qd), lambda i, hh: (i, hh)),
                  pl.BlockSpec((1, 2, nk, qd), lambda i, hh: (hh, 0, 0, 0))],
        out_specs=[tab_spec] * 4,
        out_shape=[tab, tab, tab_bf16, tab_bf16],
        compiler_params=_params("arbitrary", "arbitrary"),
        name="peer_select",
    )(q, keys)


def _peer_expert_kernel(h_ref, down_ref, up_ref, e0_ref, c0_ref, e1_ref, r1_ref, o_ref, *, heads, nk):
    e = pl.program_id(1)
    eb = down_ref.shape[0]

    @pl.when(e == 0)
    def _():
        o_ref[...] = jnp.zeros_like(o_ref)

    act = lax.dot_general(down_ref[...], h_ref[...], NT_DIMS, preferred_element_type=F32)
    act = _gelu_tanh(act).astype(BF16)
    w_rows = []
    for il in range(eb // nk):
        i = e * (eb // nk) + il
        w = None
        for h in range(heads):
            row = pl.ds(h * nk + i, 1)
            blk = slice(h * nk, (h + 1) * nk)
            count = c0_ref[row, :].astype(BF16)
            gate0 = e0_ref[row, :].astype(BF16)
            contrib = jnp.where(r1_ref[blk, :] < count, gate0 * e1_ref[blk, :], jnp.zeros((), BF16))
            w = contrib if w is None else w + contrib
        w_rows.append(w)
    xt = jnp.concatenate(w_rows, axis=0) * act
    o_ref[...] += lax.dot_general(xt, up_ref[...], TN_DIMS, preferred_element_type=F32)


def peer_experts(h, down, up, tables, *, layer, heads, nk, tb=512, eb=512):
    t, d = h.shape
    n_exp = down.shape[1]
    tb, eb = min(tb, t), min(eb, n_exp)
    tab_spec = pl.BlockSpec((heads * nk, tb), lambda i, e: (0, i))
    return pl.pallas_call(
        functools.partial(_peer_expert_kernel, heads=heads, nk=nk),
        grid=(t // tb, n_exp // eb),
        in_specs=[pl.BlockSpec((tb, d), lambda i, e: (i, 0)),
                  pl.BlockSpec((None, eb, d), lambda i, e: (layer, e, 0)),
                  pl.BlockSpec((None, eb, d), lambda i, e: (layer, e, 0))] + [tab_spec] * 4,
        out_specs=pl.BlockSpec((tb, d), lambda i, e: (i, 0)),
        out_shape=jax.ShapeDtypeStruct((t, d), F32),
        compiler_params=_params("arbitrary", "arbitrary"),
        name="peer_experts",
    )(h, down, up, *tables)


def kernel(x, mem, norm_mix, w_in, conv_w, conv_b, lru_wr, lru_br, lru_wi, lru_bi, lru_lambda, gla_w_lr, gla_b_gate, gla_norm, w_pa, w_pb, w_pc, w_out, norm_cross, norm_mem, x_wq, x_wk, x_wv, x_wo, norm_ffn, peer_wq, peer_keys, peer_down, peer_up, norm_final):
    depth = w_in.shape[0]
    batch, seq, d = x.shape
    assert batch == 1
    width = conv_w.shape[2]
    heads = width // HEAD_DIM
    rank = gla_w_lr.shape[1]
    n_main = 9 * width
    col = lambda k: k * heads

    xs = x.reshape(seq, d)
    mem2 = mem.reshape(mem.shape[1], d)
    w_in_t = jnp.swapaxes(w_in, 1, 2).astype(BF16)
    peer_down_bf16 = peer_down.astype(BF16)
    peer_up_bf16 = peer_up.astype(BF16)
    deltas = []
    for l in range(depth):
        w_lr = jnp.pad(w_in_t[l, n_main:n_main + rank], ((0, LANES - rank), (0, 0)))
        w_gates = w_in_t[l, n_main + rank:]

        if l == 0:
            h = resnorm(xs, [], norm_mix[l], emit_x=False, out_dtype=BF16)
        else:
            xs, h = resnorm(xs, deltas, norm_mix[l], emit_x=True, out_dtype=BF16)
        proj = matmul(h, w_in_t, BF16, transposed=True, layer=l, n=n_main)
        lr = matmul(h, w_lr, BF16, bn=LANES, transposed=True)
        gates = matmul(h, w_gates, BF16, transposed=True)

        ya = lru_branch(proj, conv_w[l], conv_b[l], lru_wr[l], lru_br[l], lru_wi[l], lru_bi[l],
                        lru_lambda[l], width=width)
        yb = stick_breaking(proj, heads=heads, q_col=col(2), k_col=col(3), v_col=col(4))
        w_lr_gate = jnp.pad(gla_w_lr[l], ((0, LANES - rank), (0, 0))).astype(BF16)
        yc = gla_branch(proj, lr, w_lr_gate, gla_b_gate[l], gla_norm[l], heads=heads,
                        q_col=col(5), k_col=col(6), v_col=col(7), gate_col=col(8))
        mixed = merge_branches(ya, yb, yc, w_pa[l].astype(BF16), w_pb[l].astype(BF16),
                               w_pc[l].astype(BF16), gates)
        delta = matmul(mixed, w_out[l].astype(BF16), F32)

        xs, h = resnorm(xs, [delta], norm_cross[l], emit_x=True, out_dtype=BF16)
        m = resnorm(mem2, [], norm_mem[l], emit_x=False, out_dtype=BF16)
        q = matmul(h, x_wq[l].astype(BF16), BF16)
        k = matmul(m, x_wk[l].astype(BF16), BF16)
        v = matmul(m, x_wv[l].astype(BF16), BF16)
        o = cross_attention(q, k, v, heads=X_HEADS)
        delta = matmul(o, x_wo[l].astype(BF16), F32)

        xs, h = resnorm(xs, [delta], norm_ffn[l], emit_x=True, out_dtype=BF16)
        pq = matmul(h, peer_wq[l].astype(BF16), F32)
        tables = peer_select(pq, peer_keys[l])
        delta = peer_experts(h, peer_down_bf16, peer_up_bf16, tables, layer=l,
                             heads=peer_keys.shape[1], nk=peer_keys.shape[3])
        deltas = [delta]

    out = resnorm(xs, deltas, norm_final, emit_x=False, out_dtype=F32)
    return out.reshape(batch, seq, d)
```
